```python
import math
import jax
import jax.numpy as jnp
from jax import lax
import numpy as np

D_MODEL = 2048
BATCH = 4
SEQ = 4096
DEPTH = 4

GRID_W = 64
CTX_LEN = 256
N_MOD = 6
EPS = 1e-6

NA_HEAD_DIM = 128
NA_W = D_MODEL // 2
NA_HEADS = NA_W // NA_HEAD_DIM
NA_ROWS = 8
NA_COLS = 16

SGU_W = D_MODEL // 4
SGU_HEADS = 4
SGU_HEAD_DIM = SGU_W // SGU_HEADS
SGU_CHUNK = 128

S5_W = D_MODEL // 4
S5_GROUP = 16
S5_GROUPS = S5_W // S5_GROUP
S5_STATE = 64
DT_MIN = 1e-3
DT_MAX = 1e-1

MIX_W = NA_W + SGU_W + S5_W
IN_W = 3 * NA_W + 2 * SGU_W + S5_W
COL_SPLITS = (NA_W, 2 * NA_W, 3 * NA_W, 3 * NA_W + SGU_W, 3 * NA_W + 2 * SGU_W)

PEER_HEADS = 8
PEER_QDIM = 256
PEER_NKEYS = 128
PEER_TOPK = 16
PEER_EXPERTS = PEER_NKEYS * PEER_NKEYS
PEER_CHUNK = 128

kernel_name = "hybrid_na_sgu_s5_peer_dit"

F32 = jnp.float32


def rms_norm(x, g):
    xf = x.astype(F32)
    y = xf * lax.rsqrt(jnp.mean(xf * xf, axis=-1, keepdims=True) + EPS)
    return (y * g.astype(F32)).astype(x.dtype)


def modulate(h, shift, scale):
    return h * (1 + scale) + shift


def gelu(x):
    return jax.nn.gelu(x, approximate=False)


def neighbourhood_attention(q, k, v, kc, vc, rpb):
    B, L, H, Dh = q.shape
    rows = L // GRID_W
    kr = min(NA_ROWS, rows)
    scale = Dh ** -0.5
    qg = q.reshape(B, rows, GRID_W, H, Dh)
    kg = k.reshape(B, rows, GRID_W, H, Dh)
    vg = v.reshape(B, rows, GRID_W, H, Dh)
    col = jnp.arange(GRID_W)
    col_start = jnp.clip(col - NA_COLS // 2, 0, GRID_W - NA_COLS)
    col_idx = col_start[:, None] + jnp.arange(NA_COLS)
    col_off = col_idx - col[:, None] + (NA_COLS - 1)

    def one_row(r):
        r_start = jnp.clip(r - kr // 2, 0, rows - kr)
        q_r = lax.dynamic_index_in_dim(qg, r, axis=1, keepdims=False)
        k_band = lax.dynamic_slice_in_dim(kg, r_start, kr, axis=1)
        v_band = lax.dynamic_slice_in_dim(vg, r_start, kr, axis=1)
        k_win = k_band[:, :, col_idx]
        v_win = v_band[:, :, col_idx]
        row_off = r_start + jnp.arange(kr) - r + (NA_ROWS - 1)
        bias = rpb[:, row_off[:, None, None], col_off[None, :, :]]
        bias = jnp.transpose(bias, (0, 2, 1, 3)).astype(F32)
        s_loc = jnp.einsum("bchd,brcwhd->bhcrw", q_r, k_win).astype(F32) * scale + bias[None]
        s_ctx = jnp.einsum("bchd,bnhd->bhcn", q_r, kc).astype(F32) * scale
        s = jnp.concatenate([s_loc.reshape(B, H, GRID_W, kr * NA_COLS), s_ctx], axis=-1)
        p = jax.nn.softmax(s, axis=-1)
        p_loc = p[..., : kr * NA_COLS].reshape(B, H, GRID_W, kr, NA_COLS).astype(v.dtype)
        p_ctx = p[..., kr * NA_COLS:].astype(v.dtype)
        return (jnp.einsum("bhcrw,brcwhd->bchd", p_loc, v_win)
                + jnp.einsum("bhcn,bnhd->bchd", p_ctx, vc))

    out = lax.map(one_row, jnp.arange(rows))
    return jnp.transpose(out, (1, 0, 2, 3, 4)).reshape(B, L, H * Dh)


def context_attention(qc, kc, vc):
    s = jnp.einsum("bqhd,bkhd->bhqk", qc, kc).astype(F32) * NA_HEAD_DIM ** -0.5
    p = jax.nn.softmax(s, axis=-1).astype(vc.dtype)
    o = jnp.einsum("bhqk,bkhd->bqhd", p, vc)
    return o.reshape(o.shape[0], o.shape[1], NA_W)


def spatial_gating(su, sv, w_s, b_s, g):
    B, L, _ = su.shape
    u = gelu(su)
    v = rms_norm(gelu(sv), g).reshape(B, L // SGU_CHUNK, SGU_CHUNK, SGU_HEADS, SGU_HEAD_DIM)
    mixed = jnp.einsum("hts,bnshc->bnthc", w_s, v) + b_s.T[:, :, None]
    return u * mixed.reshape(B, L, SGU_W)


def zoh_discretize(a_re, a_im, log_dt, b_re, b_im):
    lam = lax.complex(jnp.minimum(a_re.astype(F32), -1e-4), a_im.astype(F32))
    dt = jnp.exp(log_dt.astype(F32))[:, None]
    lam_bar = jnp.exp(lam * dt)
    b = lax.complex(b_re.astype(F32), b_im.astype(F32))
    b_bar = ((lam_bar - 1) / lam)[..., None] * b
    return lam_bar, b_bar


def _linear_recurrence(e1, e2):
    a1, b1 = e1
    a2, b2 = e2
    return a1 * a2, a2 * b1 + b2


def diag_scan(lam_bar, drive, h0, reverse):
    if h0 is not None:
        edge = drive.shape[1] - 1 if reverse else 0
        drive = drive.at[:, edge].add(lam_bar * h0)
    decay = jnp.broadcast_to(lam_bar, drive.shape)
    _, h = lax.associative_scan(_linear_recurrence, (decay, drive), reverse=reverse, axis=1)
    return h


def s5_mixer(s_in, s_ctx, a_re, a_im, log_dt, b_re, b_im, c_re, c_im, d, glu_w, glu_b, ctx_out):
    def to_groups(t):
        return t.astype(F32).reshape(t.shape[0], t.shape[1], S5_GROUPS, S5_GROUP)

    def drive(b_bar, u):
        return jnp.einsum("gph,blgh->blgp", b_bar, u.astype(jnp.complex64))

    def readout(c_mat, h):
        return jnp.real(jnp.einsum("ghp,blgp->blgh", c_mat, h))

    def glu(y):
        y = gelu(y.reshape(y.shape[0], y.shape[1], S5_W))
        return y * jax.nn.sigmoid(y @ glu_w.astype(F32) + glu_b.astype(F32))

    u, uc = to_groups(s_in), to_groups(s_ctx)
    d_g = d.astype(F32).reshape(S5_GROUPS, S5_GROUP)
    y = u * d_g
    yc = uc * d_g if ctx_out else None
    for direction in range(2):
        reverse = direction == 1
        lam_bar, b_bar = zoh_discretize(a_re[direction], a_im[direction], log_dt[direction],
                                        b_re[direction], b_im[direction])
        c_mat = lax.complex(c_re[direction].astype(F32), c_im[direction].astype(F32))
        h_ctx = diag_scan(lam_bar, drive(b_bar, uc), None, reverse)
        h_carry = h_ctx[:, 0] if reverse else h_ctx[:, -1]
        h_lat = diag_scan(lam_bar, drive(b_bar, u), h_carry, reverse)
        y = y + readout(c_mat, h_lat)
        if ctx_out:
            yc = yc + readout(c_mat, h_ctx)
    out = glu(y).astype(s_in.dtype)
    out_c = glu(yc).astype(s_in.dtype) if ctx_out else None
    return out, out_c


def merge_groups(attn, sgu, s5, out_norm, w_out):
    y = jnp.concatenate([
        rms_norm(attn, out_norm[:NA_W]),
        rms_norm(sgu, out_norm[NA_W:NA_W + SGU_W]),
        rms_norm(s5, out_norm[NA_W + SGU_W:]),
    ], axis=-1)
    return y @ w_out


def token_mixers(h, hc, w_in, w_out, out_norm, na_rpb, sgu_w, sgu_b, sgu_norm,
                 s5_a_re, s5_a_im, s5_log_dt, s5_b_re, s5_b_im, s5_c_re, s5_c_im,
                 s5_d, s5_glu_w, s5_glu_b, ctx_out):
    B, L, _ = h.shape

    def heads(t):
        return t.reshape(t.shape[0], t.shape[1], NA_HEADS, NA_HEAD_DIM)

    q, k, v, su, sv, s_in = jnp.split(h @ w_in, COL_SPLITS, axis=-1)
    if ctx_out:
        qc, kc, vc, suc, svc, s_inc = jnp.split(hc @ w_in, COL_SPLITS, axis=-1)
    else:
        kc, vc = jnp.split(hc @ w_in[:, NA_W:3 * NA_W], 2, axis=-1)
        s_inc = hc @ w_in[:, IN_W - S5_W:]
    kc_h, vc_h = heads(kc), heads(vc)

    attn = neighbourhood_attention(heads(q), heads(k), heads(v), kc_h, vc_h, na_rpb)
    sgu = spatial_gating(su, sv, sgu_w, sgu_b, sgu_norm)
    s5, s5c = s5_mixer(s_in, s_inc, s5_a_re, s5_a_im, s5_log_dt, s5_b_re, s5_b_im,
                       s5_c_re, s5_c_im, s5_d, s5_glu_w, s5_glu_b, ctx_out)
    y = merge_groups(attn, sgu, s5, out_norm, w_out)
    if not ctx_out:
        return y, None
    attn_c = context_attention(heads(qc), kc_h, vc_h)
    sgu_c = spatial_gating(suc, svc, sgu_w, sgu_b, sgu_norm)
    yc = merge_groups(attn_c, sgu_c, s5c, out_norm, w_out)
    return y, yc


def peer_ffn(h, w_q, k1, k2, u_tab, v_tab):
    B, L, D = h.shape
    tokens = h.reshape(B * L // PEER_CHUNK, PEER_CHUNK, D)
    half = PEER_QDIM // 2

    def block(xb):
        q = (xb @ w_q).reshape(PEER_CHUNK, PEER_HEADS, PEER_QDIM)
        s1 = jnp.einsum("thd,hkd->thk", q[..., :half], k1).astype(F32)
        s2 = jnp.einsum("thd,hkd->thk", q[..., half:], k2).astype(F32)
        v1, i1 = lax.top_k(s1, PEER_TOPK)
        v2, i2 = lax.top_k(s2, PEER_TOPK)
        cand_s = (v1[..., :, None] + v2[..., None, :]).reshape(PEER_CHUNK, PEER_HEADS, PEER_TOPK * PEER_TOPK)
        cand_e = (i1[..., :, None] * PEER_NKEYS + i2[..., None, :]).reshape(PEER_CHUNK, PEER_HEADS, PEER_TOPK * PEER_TOPK)
        best_s, pos = lax.top_k(cand_s, PEER_TOPK)
        expert = jnp.take_along_axis(cand_e, pos, axis=-1)
        gate = jax.nn.softmax(best_s, axis=-1)
        u = jnp.take(u_tab, expert, axis=0)
        act = gelu(jnp.einsum("thkd,td->thk", u, xb).astype(F32))
        w = (gate * act).astype(xb.dtype)
        return jnp.einsum("thk,thkd->td", w, jnp.take(v_tab, expert, axis=0))

    return lax.map(block, tokens).reshape(B, L, D)


def setup_inputs(seed: int = 0) -> dict:
    key = jax.random.key(seed)
    ks = jax.random.split(key, 32)
    D = D_MODEL
    G, P, H = S5_GROUPS, S5_STATE, S5_GROUP

    def nrm(k, shape, s):
        return jax.random.normal(k, shape, F32) * s

    return {
        "x": nrm(ks[0], (BATCH, SEQ, D), 1.0),
        "c": nrm(ks[1], (BATCH, D), 1.0),
        "ctx": nrm(ks[2], (BATCH, CTX_LEN, D), 1.0),
        "c_ctx": nrm(ks[3], (D,), 1.0),
        "ada_w": nrm(ks[4], (DEPTH, D, N_MOD * D), 0.5 * D ** -0.5),
        "ada_b": nrm(ks[5], (DEPTH, N_MOD * D), 0.01),
        "norm_mix": 1.0 + nrm(ks[6], (DEPTH, D), 0.01),
        "norm_ffn": 1.0 + nrm(ks[7], (DEPTH, D), 0.01),
        "w_in": nrm(ks[8], (DEPTH, D, IN_W), D ** -0.5),
        "w_out": nrm(ks[9], (DEPTH, MIX_W, D), MIX_W ** -0.5),
        "out_norm": 1.0 + nrm(ks[10], (DEPTH, MIX_W), 0.01),
        "na_rpb": nrm(ks[11], (DEPTH, NA_HEADS, 2 * NA_ROWS - 1, 2 * NA_COLS - 1), 0.02),
        "sgu_w": nrm(ks[12], (DEPTH, SGU_HEADS, SGU_CHUNK, SGU_CHUNK), 0.1 * SGU_CHUNK ** -0.5),
        "sgu_b": 1.0 + nrm(ks[13], (DEPTH, SGU_HEADS, SGU_CHUNK), 0.1),
        "sgu_norm": 1.0 + nrm(ks[14], (DEPTH, SGU_W), 0.01),
        "s5_a_re": -0.5 + nrm(ks[15], (DEPTH, 2, G, P), 0.01),
        "s5_a_im": math.pi * jnp.arange(P, dtype=F32) + nrm(ks[16], (DEPTH, 2, G, P), 0.01),
        "s5_log_dt": jax.random.uniform(ks[17], (DEPTH, 2, G), F32, math.log(DT_MIN), math.log(DT_MAX)),
        "s5_b_re": nrm(ks[18], (DEPTH, 2, G, P, H), (2 * H) ** -0.5),
        "s5_b_im": nrm(ks[19], (DEPTH, 2, G, P, H), (2 * H) ** -0.5),
        "s5_c_re": nrm(ks[20], (DEPTH, 2, G, H, P), (2 * P) ** -0.5),
        "s5_c_im": nrm(ks[21], (DEPTH, 2, G, H, P), (2 * P) ** -0.5),
        "s5_d": nrm(ks[22], (DEPTH, S5_W), 1.0),
        "s5_glu_w": nrm(ks[23], (DEPTH, S5_W, S5_W), S5_W ** -0.5),
        "s5_glu_b": nrm(ks[24], (DEPTH, S5_W), 0.01),
        "peer_wq": nrm(ks[25], (DEPTH, D, PEER_HEADS * PEER_QDIM), D ** -0.5),
        "peer_k1": nrm(ks[26], (DEPTH, PEER_HEADS, PEER_NKEYS, PEER_QDIM // 2), (PEER_QDIM // 2) ** -0.5),
        "peer_k2": nrm(ks[27], (DEPTH, PEER_HEADS, PEER_NKEYS, PEER_QDIM // 2), (PEER_QDIM // 2) ** -0.5),
        "peer_u": nrm(ks[28], (DEPTH, PEER_EXPERTS, D), D ** -0.5),
        "peer_v": nrm(ks[29], (DEPTH, PEER_EXPERTS, D), PEER_HEADS ** -0.5),
        "final_norm": 1.0 + nrm(ks[30], (D,), 0.01),
    }


def reference(x, c, ctx, c_ctx, ada_w, ada_b, norm_mix, norm_ffn, w_in, w_out, out_norm,
              na_rpb, sgu_w, sgu_b, sgu_norm, s5_a_re, s5_a_im, s5_log_dt, s5_b_re, s5_b_im,
              s5_c_re, s5_c_im, s5_d, s5_glu_w, s5_glu_b, peer_wq, peer_k1, peer_k2,
              peer_u, peer_v, final_norm):
    B = x.shape[0]
    D = x.shape[-1]
    sc = jax.nn.silu(c)
    scc = jax.nn.silu(c_ctx)
    xc = ctx
    for i in range(DEPTH):
        ctx_out = i < DEPTH - 1
        mod = (sc @ ada_w[i] + ada_b[i]).reshape(B, N_MOD, D)[:, :, None, :]
        modc = (scc @ ada_w[i] + ada_b[i]).reshape(N_MOD, D)

        h = modulate(rms_norm(x, norm_mix[i]), mod[:, 0], mod[:, 1])
        hc = modulate(rms_norm(xc, norm_mix[i]), modc[0], modc[1])
        y, yc = token_mixers(h, hc, w_in[i], w_out[i], out_norm[i], na_rpb[i],
                             sgu_w[i], sgu_b[i], sgu_norm[i],
                             s5_a_re[i], s5_a_im[i], s5_log_dt[i], s5_b_re[i], s5_b_im[i],
                             s5_c_re[i], s5_c_im[i], s5_d[i], s5_glu_w[i], s5_glu_b[i], ctx_out)
        x = x + mod[:, 2] * y

        h = modulate(rms_norm(x, norm_ffn[i]), mod[:, 3], mod[:, 4])
        x = x + mod[:, 5] * peer_ffn(h, peer_wq[i], peer_k1[i], peer_k2[i], peer_u[i], peer_v[i])

        if ctx_out:
            xc = xc + modc[2] * yc
            hc = modulate(rms_norm(xc, norm_ffn[i]), modc[3], modc[4])
            xc = xc + modc[5] * peer_ffn(hc, peer_wq[i], peer_k1[i], peer_k2[i], peer_u[i], peer_v[i])
    return rms_norm(x, final_norm)
```

```python
import functools
import math

import numpy as np
import jax
import jax.numpy as jnp
from jax import lax
from jax.experimental import pallas as pl
from jax.experimental.pallas import tpu as pltpu

F32 = jnp.float32
BF16 = jnp.bfloat16
EPS = 1e-6
GRID_W = 64
PEER_TOPK = 16
S5_T = 64
NEG = -1e30
HI = lax.Precision.HIGHEST
VMEM_LIMIT = 56 * 1024 * 1024


def _cp(sem, vmem=VMEM_LIMIT):
    return pltpu.CompilerParams(dimension_semantics=sem, vmem_limit_bytes=vmem)


def _tile(n, cap, unit):
    return max(t for t in range(unit, min(n, cap) + 1, unit) if n % t == 0)


def _gelu(x):
    return 0.5 * x * (1.0 + lax.erf(x * (1.0 / math.sqrt(2.0))))


def _dot_t(a, b, precision=None):
    return lax.dot_general(a, b, (((1,), (1,)), ((), ())), precision=precision,
                           preferred_element_type=F32)


def _dot(a, b, precision=None):
    return lax.dot_general(a, b, (((1,), (0,)), ((), ())), precision=precision,
                           preferred_element_type=F32)


def _mod_row(i, n_lat_blocks, blocks_per_batch, n_batch):
    return jnp.where(i < n_lat_blocks, i // blocks_per_batch, n_batch)


def _ada_kernel(c_ref, w_ref, b_ref, o_ref):
    c = c_ref[...]
    s = (c * jax.nn.sigmoid(c)).astype(BF16)
    o_ref[0] = _dot(s, w_ref[0].astype(BF16)) + b_ref[0]


def _ada_mod(cond, ada_w, ada_b, tn=1024):
    depth, d, nm = ada_w.shape
    rows = cond.shape[0]
    return pl.pallas_call(
        _ada_kernel,
        grid=(depth, nm // tn),
        in_specs=[pl.BlockSpec((rows, d), lambda l, j: (0, 0)),
                  pl.BlockSpec((1, d, tn), lambda l, j: (l, 0, j)),
                  pl.BlockSpec((1, 1, tn), lambda l, j: (l, 0, j))],
        out_specs=pl.BlockSpec((1, rows, tn), lambda l, j: (l, 0, j)),
        out_shape=jax.ShapeDtypeStruct((depth, rows, nm), F32),
        compiler_params=_cp(("arbitrary", "arbitrary")),
        name="ada_mod",
    )(cond, ada_w, ada_b.reshape(depth, 1, nm))


def _norm_mod_kernel(x_ref, g_ref, m_ref, o_ref, *, row):
    x = x_ref[...]
    m = m_ref[0]
    y = x * lax.rsqrt(jnp.mean(x * x, axis=-1, keepdims=True) + EPS) * g_ref[...]
    o_ref[...] = (y * (1.0 + m[row + 1:row + 2]) + m[row:row + 1]).astype(o_ref.dtype)


def _norm_mod(x, g, mod, row, lat_blocks, tm):
    n, d = x.shape
    nlb, bpb, nb = lat_blocks
    return pl.pallas_call(
        functools.partial(_norm_mod_kernel, row=row),
        grid=(n // tm,),
        in_specs=[pl.BlockSpec((tm, d), lambda i: (i, 0)),
                  pl.BlockSpec((1, d), lambda i: (0, 0)),
                  pl.BlockSpec((1, mod.shape[1], d), lambda i: (_mod_row(i, nlb, bpb, nb), 0, 0))],
        out_specs=pl.BlockSpec((tm, d), lambda i: (i, 0)),
        out_shape=jax.ShapeDtypeStruct((n, d), BF16),
        compiler_params=_cp(("arbitrary",)),
        name="norm_mod",
    )(x, g.reshape(1, d), mod)


def _mm_kernel(a_ref, w_ref, o_ref):
    o_ref[...] = _dot(a_ref[...], w_ref[...])


def _mm_res_kernel(a_ref, w_ref, r_ref, m_ref, o_ref, *, row):
    o_ref[...] = r_ref[...] + m_ref[0][row:row + 1] * _dot(a_ref[...], w_ref[...])


def _matmul(a, w, tm, tn):
    m, k = a.shape
    nn = w.shape[1]
    return pl.pallas_call(
        _mm_kernel,
        grid=(nn // tn, m // tm),
        in_specs=[pl.BlockSpec((tm, k), lambda j, i: (i, 0)),
                  pl.BlockSpec((k, tn), lambda j, i: (0, j))],
        out_specs=pl.BlockSpec((tm, tn), lambda j, i: (i, j)),
        out_shape=jax.ShapeDtypeStruct((m, nn), F32),
        compiler_params=_cp(("arbitrary", "arbitrary")),
        name="matmul",
    )(a, w)


def _matmul_res(a, w, res, mod, row, lat_blocks, tm, tn):
    m, k = a.shape
    nn = w.shape[1]
    nlb, bpb, nb = lat_blocks
    return pl.pallas_call(
        functools.partial(_mm_res_kernel, row=row),
        grid=(nn // tn, m // tm),
        in_specs=[pl.BlockSpec((tm, k), lambda j, i: (i, 0)),
                  pl.BlockSpec((k, tn), lambda j, i: (0, j)),
                  pl.BlockSpec((tm, tn), lambda j, i: (i, j)),
                  pl.BlockSpec((1, mod.shape[1], tn), lambda j, i: (_mod_row(i, nlb, bpb, nb), 0, j))],
        out_specs=pl.BlockSpec((tm, tn), lambda j, i: (i, j)),
        out_shape=jax.ShapeDtypeStruct((m, nn), F32),
        compiler_params=_cp(("arbitrary", "arbitrary")),
        name="matmul_res",
    )(a, w, res, mod)


def _na_tables(rows, width, kr, wc, rb_rows):
    band = rb_rows + kr
    nrb = rows // rb_rows
    masked = 2 * kr - 1
    ro = np.full((nrb, rb_rows, band), masked, dtype=np.int32)
    bstart = np.zeros((nrb,), dtype=np.int32)
    for rb in range(nrb):
        bs = int(np.clip(rb_rows * rb - kr // 2, 0, rows - band))
        bstart[rb] = bs
        for dlt in range(rb_rows):
            r = rb_rows * rb + dlt
            rs = int(np.clip(r - kr // 2, 0, rows - kr))
            for j in range(kr):
                ro[rb, dlt, rs + j - bs] = rs + j - r + (kr - 1)
    types, inv = np.unique(ro.reshape(nrb, -1), axis=0, return_inverse=True)
    types = types.reshape(-1, rb_rows, band)
    col = np.arange(width)
    cs = np.clip(col - wc // 2, 0, width - wc)
    kc = np.arange(width)
    valid = (kc[None, :] >= cs[:, None]) & (kc[None, :] < cs[:, None] + wc)
    co = np.clip(kc[None, :] - col[:, None] + (wc - 1), 0, 2 * wc - 2)
    return types, inv.reshape(-1).astype(np.int32), bstart, co, valid


def _na_bias(rpb, types, co, valid):
    h = rpb.shape[0]
    w = co.shape[0]
    tb = jnp.where(valid[None, None], rpb[:, :, co], NEG)
    tb = jnp.concatenate([tb, jnp.full((h, 1, w, w), NEG, F32)], axis=1)
    dense = tb[:, types]
    nt, rbr, band = types.shape
    dense = jnp.transpose(dense, (1, 0, 2, 4, 3, 5))
    return dense.reshape(nt, h, rbr * w, band * w)


def _na_kernel(type_ref, bstart_ref, q_ref, k_ref, v_ref, kc_ref, vc_ref, bias_ref, o_ref, *,
               band_tokens, width, scale):
    rb = pl.program_id(2)
    start = pl.multiple_of(bstart_ref[rb] * width, width)
    q = q_ref[...].astype(BF16)
    kb = k_ref[pl.ds(start, band_tokens), :].astype(BF16)
    vb = v_ref[pl.ds(start, band_tokens), :].astype(BF16)
    s_loc = _dot_t(q, kb) * scale + bias_ref[0, 0]
    s_ctx = _dot_t(q, kc_ref[...].astype(BF16)) * scale
    m = jnp.maximum(jnp.max(s_loc, axis=-1, keepdims=True), jnp.max(s_ctx, axis=-1, keepdims=True))
    p_loc = jnp.exp(s_loc - m)
    p_ctx = jnp.exp(s_ctx - m)
    den = jnp.sum(p_loc, axis=-1, keepdims=True) + jnp.sum(p_ctx, axis=-1, keepdims=True)
    o = _dot(p_loc.astype(BF16), vb) + _dot(p_ctx.astype(BF16), vc_ref[...].astype(BF16))
    o_ref[...] = o / den


def _na_attention(proj, bias, rb_type, bstart, n_batch, seq, ctx_len, heads, dh, kr, rb_rows):
    width = GRID_W
    nrb = seq // (rb_rows * width)
    qt = rb_rows * width
    band_tokens = (rb_rows + kr) * width
    ctx_blk0 = n_batch * seq // ctx_len
    grid_spec = pltpu.PrefetchScalarGridSpec(
        num_scalar_prefetch=2,
        grid=(n_batch, heads, nrb),
        in_specs=[
            pl.BlockSpec((qt, dh), lambda b, h, r, t, s: (b * nrb + r, h)),
            pl.BlockSpec((seq, dh), lambda b, h, r, t, s: (b, heads + h)),
            pl.BlockSpec((seq, dh), lambda b, h, r, t, s: (b, 2 * heads + h)),
            pl.BlockSpec((ctx_len, dh), lambda b, h, r, t, s: (ctx_blk0 + b, heads + h)),
            pl.BlockSpec((ctx_len, dh), lambda b, h, r, t, s: (ctx_blk0 + b, 2 * heads + h)),
            pl.BlockSpec((1, 1, qt, band_tokens), lambda b, h, r, t, s: (t[r], h, 0, 0)),
        ],
        out_specs=pl.BlockSpec((qt, dh), lambda b, h, r, t, s: (b * nrb + r, h)),
    )
    return pl.pallas_call(
        functools.partial(_na_kernel, band_tokens=band_tokens, width=width, scale=dh ** -0.5),
        grid_spec=grid_spec,
        out_shape=jax.ShapeDtypeStruct((n_batch * seq, heads * dh), F32),
        compiler_params=_cp(("arbitrary", "arbitrary", "arbitrary")),
        name="na_attention",
    )(rb_type, bstart, proj, proj, proj, proj, proj, bias)


def _ctx_attn_kernel(q_ref, k_ref, v_ref, o_ref, *, scale):
    s = _dot_t(q_ref[...].astype(BF16), k_ref[...].astype(BF16)) * scale
    m = jnp.max(s, axis=-1, keepdims=True)
    p = jnp.exp(s - m)
    den = jnp.sum(p, axis=-1, keepdims=True)
    o_ref[...] = _dot(p.astype(BF16), v_ref[...].astype(BF16)) / den


def _ctx_attention(proj, n_batch, seq, ctx_len, heads, dh):
    ctx_blk0 = n_batch * seq // ctx_len
    return pl.pallas_call(
        functools.partial(_ctx_attn_kernel, scale=dh ** -0.5),
        grid=(n_batch, heads),
        in_specs=[pl.BlockSpec((ctx_len, dh), lambda b, h: (ctx_blk0 + b, h)),
                  pl.BlockSpec((ctx_len, dh), lambda b, h: (ctx_blk0 + b, heads + h)),
                  pl.BlockSpec((ctx_len, dh), lambda b, h: (ctx_blk0 + b, 2 * heads + h))],
        out_specs=pl.BlockSpec((ctx_len, dh), lambda b, h: (b, h)),
        out_shape=jax.ShapeDtypeStruct((n_batch * ctx_len, heads * dh), F32),
        compiler_params=_cp(("arbitrary", "arbitrary")),
        name="ctx_attention",
    )(proj, proj, proj)


def _sgu_kernel(su_ref, sv_ref, w_ref, b_ref, g_ref, o_ref, *, chunk, heads, hd):
    u = _gelu(su_ref[...])
    gv = _gelu(sv_ref[...])
    v = (gv * lax.rsqrt(jnp.mean(gv * gv, axis=-1, keepdims=True) + EPS) * g_ref[...]).astype(BF16)
    for n in range(u.shape[0] // chunk):
        rs = slice(n * chunk, (n + 1) * chunk)
        for h in range(heads):
            cs = slice(h * hd, (h + 1) * hd)
            mixed = _dot(w_ref[h], v[rs, cs]) + b_ref[h]
            o_ref[rs, cs] = u[rs, cs] * mixed


def _sgu(proj, sgu_w, sgu_b, sgu_norm, col_blk, tm):
    n = proj.shape[0]
    heads, chunk, _ = sgu_w.shape
    sw = sgu_norm.shape[-1]
    hd = sw // heads
    bias = jnp.broadcast_to(sgu_b[:, :, None], (heads, chunk, hd))
    return pl.pallas_call(
        functools.partial(_sgu_kernel, chunk=chunk, heads=heads, hd=hd),
        grid=(n // tm,),
        in_specs=[pl.BlockSpec((tm, sw), lambda i: (i, col_blk)),
                  pl.BlockSpec((tm, sw), lambda i: (i, col_blk + 1)),
                  pl.BlockSpec((heads, chunk, chunk), lambda i: (0, 0, 0)),
                  pl.BlockSpec((heads, chunk, hd), lambda i: (0, 0, 0)),
                  pl.BlockSpec((1, sw), lambda i: (0, 0))],
        out_specs=pl.BlockSpec((tm, sw), lambda i: (i, 0)),
        out_shape=jax.ShapeDtypeStruct((n, sw), F32),
        compiler_params=_cp(("arbitrary",)),
        name="sgu",
    )(proj, proj, sgu_w.astype(BF16), bias, sgu_norm.reshape(1, sw))


def _s5_prep_kernel(are_ref, aim_ref, ldt_ref, btr_ref, bti_ref, cr_ref, ci_ref,
                    mt_ref, bp_ref, cp_ref, lam_ref, *, t_len, n_h, n_p):
    rev = pl.program_id(1) == 1
    sgn = jnp.where(rev, -1.0, 1.0)
    are = jnp.minimum(are_ref[0, 0, 0], -1e-4)
    aim = aim_ref[0, 0, 0]
    dt = jnp.exp(ldt_ref[0, 0, 0])
    lr = are * dt
    li = aim * dt
    first = lax.broadcasted_iota(jnp.int32, (1, 2 * n_p), 1) < n_p

    def power(n):
        mag = jnp.exp(n * lr)
        ang = n * li
        return mag * jnp.cos(ang), mag * jnp.sin(ang)

    lbr = jnp.exp(lr) * jnp.cos(li)
    lbi = jnp.exp(lr) * jnp.sin(li)
    den = are * are + aim * aim
    nr = lbr - 1.0
    kr_ = (nr * are + lbi * aim) / den
    ki_ = (lbi * are - nr * aim) / den
    btr = btr_ref[0, 0, 0]
    bti = bti_ref[0, 0, 0]
    bbr = kr_ * btr - ki_ * bti
    bbi = kr_ * bti + ki_ * btr
    ccr = cr_ref[0, 0, 0]
    cci = ci_ref[0, 0, 0]
    b_a = jnp.where(first, bbr, -bbi)
    b_b = jnp.where(first, -bbi, -bbr)
    c_a = jnp.where(first, ccr, cci)
    c_b = jnp.where(first, -cci, ccr)
    b_ri_a = jnp.where(first, bbr, bbi)
    b_ri_b = jnp.where(first, -bbi, bbr)
    c_n_a = jnp.where(first, ccr, -cci)
    c_n_b = jnp.where(first, -cci, -ccr)

    th = t_len * n_h
    tt = (lax.broadcasted_iota(jnp.int32, (th, 1), 0) // n_h).astype(F32)
    half = 0.5 * t_len

    def expand(pw, xa, xb):
        pr, pi_ = pw
        return (pr * jnp.concatenate([xa] * t_len, axis=0)
                + pi_ * jnp.concatenate([xb] * t_len, axis=0))

    a_cat = expand(power(sgn * (half - tt)), b_a, b_b)
    c_cat = expand(power(sgn * (tt - half)), c_a, c_b)
    mt = _dot_t(a_cat, c_cat, precision=HI)
    rs = lax.broadcasted_iota(jnp.int32, (th, th), 0) // n_h
    ct = lax.broadcasted_iota(jnp.int32, (th, th), 1) // n_h
    keep = (ct - rs) * jnp.where(rev, -1, 1) >= 0
    mt_ref[0, 0, 0] = jnp.where(keep, mt, 0.0).astype(mt_ref.dtype)

    n_out = jnp.where(rev, tt, (t_len - 1.0) - tt)
    bp_ref[0, 0, 0] = expand(power(n_out), b_ri_a, b_ri_b)
    n_in = jnp.where(rev, t_len - tt, tt + 1.0)
    cp_ref[0, 0, 0] = expand(power(n_in), c_n_a, c_n_b)
    ltr, lti = power(jnp.full((1, 1), float(t_len), F32))
    lam_ref[0, 0, 0, 0:1, :] = ltr
    lam_ref[0, 0, 0, 1:2, :] = jnp.where(first, -lti, lti)


def _s5_prep(a_re, a_im, log_dt, b_re, b_im, c_re, c_im, t_len):
    depth, _, g, p, h = b_re.shape
    dup = lambda t: jnp.concatenate([t, t], axis=-1)
    are = dup(a_re).reshape(depth, 2, g, 1, 2 * p)
    aim = dup(a_im).reshape(depth, 2, g, 1, 2 * p)
    ldt = log_dt.reshape(depth, 2, g, 1, 1)
    btr = dup(jnp.swapaxes(b_re, -1, -2))
    bti = dup(jnp.swapaxes(b_im, -1, -2))
    ccr = dup(c_re)
    cci = dup(c_im)
    th = t_len * h
    row = lambda r, c: pl.BlockSpec((1, 1, 1, r, c), lambda l, d, k: (l, d, k, 0, 0))
    return pl.pallas_call(
        functools.partial(_s5_prep_kernel, t_len=t_len, n_h=h, n_p=p),
        grid=(depth, 2, g),
        in_specs=[row(1, 2 * p), row(1, 2 * p), row(1, 1), row(h, 2 * p), row(h, 2 * p),
                  row(h, 2 * p), row(h, 2 * p)],
        out_specs=[row(th, th), row(th, 2 * p), row(th, 2 * p), row(2, 2 * p)],
        out_shape=[jax.ShapeDtypeStruct((depth, 2, g, th, th), BF16),
                   jax.ShapeDtypeStruct((depth, 2, g, th, 2 * p), F32),
                   jax.ShapeDtypeStruct((depth, 2, g, th, 2 * p), F32),
                   jax.ShapeDtypeStruct((depth, 2, g, 2, 2 * p), F32)],
        compiler_params=_cp(("arbitrary", "arbitrary", "arbitrary")),
        name="s5_prep",
    )(are, aim, ldt, btr, bti, ccr, cci)


def _s5_kernel(u_ref, d_ref, mt_ref, bp_ref, cp_ref, lam_ref, y_ref, hin_ref, *, n_ctx_chunks, n_p):
    u = u_ref[0]
    ub = u.astype(BF16)
    n_chunks = hin_ref.shape[0]
    y = u * d_ref[0]
    for d in range(2):
        y = y + _dot(ub, mt_ref[d, 0])
        hloc = _dot(u, bp_ref[d, 0], precision=HI).reshape(n_chunks, 8, 2 * n_p)
        lam_a = lam_ref[d, 0, 0:1, :]
        lam_b = lam_ref[d, 0, 1:2, :]
        if d == 0:
            order = list(range(n_chunks))
        else:
            order = list(range(n_ctx_chunks - 1, -1, -1)) + list(range(n_chunks - 1, n_ctx_chunks - 1, -1))
        h = jnp.zeros((8, 2 * n_p), F32)
        for c in order:
            hin_ref[c] = h
            h = h * lam_a + pltpu.roll(h, n_p, axis=1) * lam_b + hloc[c]
        hin = hin_ref[...].reshape(n_chunks * 8, 2 * n_p)
        y = y + _dot_t(hin, cp_ref[d, 0], precision=HI)
    y_ref[0] = y


def _s5_scan(u_g, d_tile, mt, bp, cp, lam, n_ctx_chunks):
    g, r, th = u_g.shape
    p2 = bp.shape[-1]
    n_chunks = r // 8
    return pl.pallas_call(
        functools.partial(_s5_kernel, n_ctx_chunks=n_ctx_chunks, n_p=p2 // 2),
        grid=(g,),
        in_specs=[pl.BlockSpec((1, r, th), lambda k: (k, 0, 0)),
                  pl.BlockSpec((1, 1, th), lambda k: (k, 0, 0)),
                  pl.BlockSpec((2, 1, th, th), lambda k: (0, k, 0, 0)),
                  pl.BlockSpec((2, 1, th, p2), lambda k: (0, k, 0, 0)),
                  pl.BlockSpec((2, 1, th, p2), lambda k: (0, k, 0, 0)),
                  pl.BlockSpec((2, 1, 2, p2), lambda k: (0, k, 0, 0))],
        out_specs=pl.BlockSpec((1, r, th), lambda k: (k, 0, 0)),
        out_shape=jax.ShapeDtypeStruct((g, r, th), F32),
        scratch_shapes=[pltpu.VMEM((n_chunks, 8, p2), F32)],
        compiler_params=_cp(("arbitrary",)),
        name="s5_scan",
    )(u_g, d_tile, mt, bp, cp, lam)


def _s5_to_groups(s_in, n_batch, seq, ctx_len, g, h, t_len):
    n_lat = n_batch * seq
    lat = s_in[:n_lat].reshape(n_batch, seq // t_len, t_len, g, h)
    cx = s_in[n_lat:].reshape(n_batch, ctx_len // t_len, t_len, g, h)
    al = jnp.concatenate([cx, lat], axis=1)
    al = jnp.transpose(al, (3, 1, 0, 2, 4))
    al = jnp.pad(al, ((0, 0), (0, 0), (0, 8 - n_batch), (0, 0), (0, 0)))
    return al.reshape(g, al.shape[1] * 8, t_len * h)


def _s5_from_groups(y_g, n_batch, seq, ctx_len, g, h, t_len):
    n_chunks = y_g.shape[1] // 8
    y = y_g.reshape(g, n_chunks, 8, t_len, h)[:, :, :n_batch]
    y = jnp.transpose(y, (2, 1, 3, 0, 4))
    ncc = ctx_len // t_len
    cx = y[:, :ncc].reshape(n_batch * ctx_len, g * h)
    lat = y[:, ncc:].reshape(n_batch * seq, g * h)
    return jnp.concatenate([lat, cx], axis=0)


def _merge_kernel(a_ref, s_ref, y_ref, gw_ref, gb_ref, n_ref, o_ref, *, wa, ws):
    def rms(t, g):
        return t * lax.rsqrt(jnp.mean(t * t, axis=-1, keepdims=True) + EPS) * g

    y = _gelu(y_ref[...])
    z = y * jax.nn.sigmoid(_dot(y.astype(BF16), gw_ref[...]) + gb_ref[...])
    o_ref[:, :wa] = rms(a_ref[...], n_ref[:, :wa]).astype(o_ref.dtype)
    o_ref[:, wa:wa + ws] = rms(s_ref[...], n_ref[:, wa:wa + ws]).astype(o_ref.dtype)
    o_ref[:, wa + ws:] = rms(z, n_ref[:, wa + ws:]).astype(o_ref.dtype)


def _merge(attn, sgu, y5, glu_w, glu_b, out_norm, tm):
    n, wa = attn.shape
    ws = sgu.shape[1]
    w5 = y5.shape[1]
    wt = wa + ws + w5
    return pl.pallas_call(
        functools.partial(_merge_kernel, wa=wa, ws=ws),
        grid=(n // tm,),
        in_specs=[pl.BlockSpec((tm, wa), lambda i: (i, 0)),
                  pl.BlockSpec((tm, ws), lambda i: (i, 0)),
                  pl.BlockSpec((tm, w5), lambda i: (i, 0)),
                  pl.BlockSpec((w5, w5), lambda i: (0, 0)),
                  pl.BlockSpec((1, w5), lambda i: (0, 0)),
                  pl.BlockSpec((1, wt), lambda i: (0, 0))],
        out_specs=pl.BlockSpec((tm, wt), lambda i: (i, 0)),
        out_shape=jax.ShapeDtypeStruct((n, wt), BF16),
        compiler_params=_cp(("arbitrary",)),
        name="merge_norm",
    )(attn, sgu, y5, glu_w.astype(BF16), glu_b.reshape(1, w5), out_norm.reshape(1, wt))


def _topk_cols(s, k, payload=None):
    n, t = s.shape
    iota = lax.broadcasted_iota(jnp.int32, (n, t), 0).astype(F32)
    krow = lax.broadcasted_iota(jnp.int32, (k, t), 0)
    vals = jnp.zeros((k, t), F32)
    outs = jnp.zeros((k, t), F32)
    for it in range(k):
        m = jnp.max(s, axis=0, keepdims=True)
        pos = jnp.min(jnp.where(s == m, iota, float(n)), axis=0, keepdims=True)
        hit = iota == pos
        if payload is None:
            sel = pos
        else:
            sel = jnp.max(jnp.where(hit, payload, -1.0), axis=0, keepdims=True)
        vals = jnp.where(krow == it, m, vals)
        outs = jnp.where(krow == it, sel, outs)
        s = jnp.where(hit, -jnp.inf, s)
    return vals, outs


def _peer_sel_kernel(h_ref, wq_ref, k1_ref, k2_ref, ids_ref, gate_ref, q_scr, *, heads, qdim, nkeys, topk, tl):
    q_scr[...] = _dot(h_ref[...], wq_ref[...])
    half = qdim // 2

    def head_body(hh, carry):
        for th in range(h_ref.shape[0] // tl):
            q1 = q_scr[pl.ds(th * tl, tl), pl.ds(pl.multiple_of(hh * qdim, qdim), half)]
            q2 = q_scr[pl.ds(th * tl, tl), pl.ds(pl.multiple_of(hh * qdim + half, half), half)]
            s1 = _dot_t(k1_ref[hh], q1, precision=HI)
            s2 = _dot_t(k2_ref[hh], q2, precision=HI)
            v1, i1 = _topk_cols(s1, topk)
            v2, i2 = _topk_cols(s2, topk)
            cand_s = jnp.concatenate([v1[a:a + 1] + v2 for a in range(topk)], axis=0)
            cand_e = jnp.concatenate([i1[a:a + 1] * float(nkeys) + i2 for a in range(topk)], axis=0)
            best, expert = _topk_cols(cand_s, topk, payload=cand_e)
            e = jnp.exp(best - best[0:1])
            gate = e / jnp.sum(e, axis=0, keepdims=True)
            rows = pl.ds(pl.multiple_of(hh * topk, topk), topk)
            ids_ref[rows, pl.ds(th * tl, tl)] = expert.astype(jnp.int32)
            gate_ref[rows, pl.ds(th * tl, tl)] = gate
        return carry

    lax.fori_loop(0, heads, head_body, 0)


def _peer_select(h, wq, k1, k2, tm, tl=128):
    n, d = h.shape
    heads, nkeys, half = k1.shape
    qdim = 2 * half
    nsel = heads * PEER_TOPK
    return pl.pallas_call(
        functools.partial(_peer_sel_kernel, heads=heads, qdim=qdim, nkeys=nkeys, topk=PEER_TOPK, tl=tl),
        grid=(n // tm,),
        in_specs=[pl.BlockSpec((tm, d), lambda i: (i, 0)),
                  pl.BlockSpec((d, heads * qdim), lambda i: (0, 0)),
                  pl.BlockSpec((heads, nkeys, half), lambda i: (0, 0, 0)),
                  pl.BlockSpec((heads, nkeys, half), lambda i: (0, 0, 0))],
        out_specs=[pl.BlockSpec((nsel, tm), lambda i: (0, i)),
                   pl.BlockSpec((nsel, tm), lambda i: (0, i))],
        out_shape=[jax.ShapeDtypeStruct((nsel, n), jnp.int32),
                   jax.ShapeDtypeStruct((nsel, n), F32)],
        scratch_shapes=[pltpu.VMEM((tm, heads * qdim), F32)],
        compiler_params=_cp(("arbitrary",)),
        name="peer_select",
    )(h, wq, k1, k2)


def _peer_ffn_kernel(ids_ref, idn_ref, g_ref, x_ref, n_ref, m_ref, tab_ref, o_ref, buf_ref, sem_ref, *,
                     tg, nsel, d, row):
    i = pl.program_id(0)
    nsteps = pl.num_programs(0)
    slot = i % 2

    def issue(src_ids, dst_slot):
        def body(p, carry):
            e = src_ids[p // nsel, p % nsel]
            pltpu.make_async_copy(tab_ref.at[pl.ds(e, 1)], buf_ref.at[dst_slot, pl.ds(p, 1)],
                                  sem_ref.at[dst_slot]).start()
            return carry
        lax.fori_loop(0, tg * nsel, body, 0, unroll=8)

    @pl.when(i == 0)
    def _():
        issue(ids_ref, 0)

    @pl.when(i + 1 < nsteps)
    def _():
        issue(idn_ref, 1 - slot)

    pltpu.make_async_copy(tab_ref.at[pl.ds(0, tg * nsel)], buf_ref.at[slot], sem_ref.at[slot]).wait()

    m = m_ref[0]
    eye = (lax.broadcasted_iota(jnp.int32, (nsel, nsel), 0)
           == lax.broadcasted_iota(jnp.int32, (nsel, nsel), 1))
    for k in range(tg):
        xk = x_ref[k:k + 1, :]
        hk = xk * lax.rsqrt(jnp.mean(xk * xk, axis=-1, keepdims=True) + EPS) * n_ref[...]
        hk = hk * (1.0 + m[row + 1:row + 2]) + m[row:row + 1]
        gu = buf_ref[slot, k * nsel:(k + 1) * nsel, :d]
        act = _gelu(jnp.sum(gu * hk, axis=1, keepdims=True))
        gcol = jnp.sum(jnp.where(eye, g_ref[k:k + 1, :], 0.0), axis=1, keepdims=True)
        gv = buf_ref[slot, k * nsel:(k + 1) * nsel, d:]
        out = jnp.sum(gv * (gcol * act), axis=0, keepdims=True)
        o_ref[k:k + 1, :] = xk + m[row + 2:row + 3] * out


def _peer_ffn(x, ids, gates, norm_g, mod, row, tab, lat_blocks, tg=8):
    n, d = x.shape
    nsel = ids.shape[1]
    nlb, bpb, nb = lat_blocks
    steps = n // tg
    return pl.pallas_call(
        functools.partial(_peer_ffn_kernel, tg=tg, nsel=nsel, d=d, row=row),
        grid=(steps,),
        in_specs=[pl.BlockSpec((tg, nsel), lambda i: (i, 0), memory_space=pltpu.SMEM),
                  pl.BlockSpec((tg, nsel), lambda i: (jnp.minimum(i + 1, steps - 1), 0),
                               memory_space=pltpu.SMEM),
                  pl.BlockSpec((tg, nsel), lambda i: (i, 0)),
                  pl.BlockSpec((tg, d), lambda i: (i, 0)),
                  pl.BlockSpec((1, d), lambda i: (0, 0)),
                  pl.BlockSpec((1, mod.shape[1], d), lambda i: (_mod_row(i, nlb, bpb, nb), 0, 0)),
                  pl.BlockSpec(memory_space=pl.ANY)],
        out_specs=pl.BlockSpec((tg, d), lambda i: (i, 0)),
        out_shape=jax.ShapeDtypeStruct((n, d), F32),
        scratch_shapes=[pltpu.VMEM((2, tg * nsel, 2 * d), F32),
                        pltpu.SemaphoreType.DMA((2,))],
        compiler_params=_cp(("arbitrary",)),
        name="peer_ffn",
    )(ids, ids, gates, x, norm_g.reshape(1, d), mod, tab)


def _final_norm_kernel(x_ref, g_ref, o_ref):
    x = x_ref[...]
    o_ref[...] = x * lax.rsqrt(jnp.mean(x * x, axis=-1, keepdims=True) + EPS) * g_ref[...]


def _final_norm(x, g, n_rows, tm):
    d = x.shape[1]
    return pl.pallas_call(
        _final_norm_kernel,
        grid=(n_rows // tm,),
        in_specs=[pl.BlockSpec((tm, d), lambda i: (i, 0)),
                  pl.BlockSpec((1, d), lambda i: (0, 0))],
        out_specs=pl.BlockSpec((tm, d), lambda i: (i, 0)),
        out_shape=jax.ShapeDtypeStruct((n_rows, d), F32),
        compiler_params=_cp(("arbitrary",)),
        name="final_norm",
    )(x, g.reshape(1, d))


def kernel(x, c, ctx, c_ctx, ada_w, ada_b, norm_mix, norm_ffn, w_in, w_out, out_norm, na_rpb, sgu_w, sgu_b,
           sgu_norm, s5_a_re, s5_a_im, s5_log_dt, s5_b_re, s5_b_im, s5_c_re, s5_c_im, s5_d, s5_glu_w,
           s5_glu_b, peer_wq, peer_k1, peer_k2, peer_u, peer_v, final_norm):
    n_batch, seq, d = x.shape
    ctx_len = ctx.shape[1]
    depth = ada_w.shape[0]
    n_mod = ada_w.shape[2] // d
    in_w = w_in.shape[2]
    sgu_width = sgu_norm.shape[-1]
    s5_width = s5_d.shape[-1]
    na_width = (in_w - 2 * sgu_width - s5_width) // 3
    na_heads = na_rpb.shape[1]
    dh = na_width // na_heads
    kr_full = (na_rpb.shape[2] + 1) // 2
    wc = (na_rpb.shape[3] + 1) // 2
    rows = seq // GRID_W
    kr = min(kr_full, rows)
    _, _, s5_g, s5_p, s5_h = s5_b_re.shape
    n_lat = n_batch * seq
    n_tok = n_lat + n_batch * ctx_len

    tm = 256
    lat_blocks = (n_lat // tm, seq // tm, n_batch)
    tg = 8
    grp_blocks = (n_lat // tg, seq // tg, n_batch)

    cond = jnp.concatenate([c, c_ctx[None], jnp.zeros((8 - n_batch - 1, d), F32)], axis=0)
    mods = _ada_mod(cond, ada_w, ada_b).reshape(depth, 8, n_mod, d)

    rb_rows = 8
    types, rb_type, bstart, co, valid = _na_tables(rows, GRID_W, kr, wc, rb_rows)
    types_ref = np.where(types == 2 * kr - 1, 2 * kr_full - 1, types + (kr_full - kr))
    s5_ops = _s5_prep(s5_a_re, s5_a_im, s5_log_dt, s5_b_re, s5_b_im, s5_c_re, s5_c_im, S5_T)

    xs = jnp.concatenate([x.reshape(n_lat, d), ctx.reshape(n_batch * ctx_len, d)], axis=0)
    for l in range(depth):
        mod = mods[l]
        h = _norm_mod(xs, norm_mix[l], mod, 0, lat_blocks, tm)
        proj = _matmul(h, w_in[l].astype(BF16), _tile(n_tok, 512, 8), _tile(in_w, 1152, 128))

        bias = _na_bias(na_rpb[l], types_ref, co, valid)
        attn_lat = _na_attention(proj, bias, jnp.asarray(rb_type), jnp.asarray(bstart), n_batch, seq, ctx_len,
                                 na_heads, dh, kr, rb_rows)
        attn_ctx = _ctx_attention(proj, n_batch, seq, ctx_len, na_heads, dh)
        attn = jnp.concatenate([attn_lat, attn_ctx], axis=0)

        sgu = _sgu(proj, sgu_w[l], sgu_b[l], sgu_norm[l], 3 * na_width // sgu_width, tm)

        s_in = proj[:, in_w - s5_width:]
        u_g = _s5_to_groups(s_in, n_batch, seq, ctx_len, s5_g, s5_h, S5_T)
        d_tile = jnp.tile(s5_d[l].reshape(s5_g, 1, s5_h), (1, 1, S5_T))
        y_g = _s5_scan(u_g, d_tile, s5_ops[0][l], s5_ops[1][l], s5_ops[2][l], s5_ops[3][l], ctx_len // S5_T)
        y5 = _s5_from_groups(y_g, n_batch, seq, ctx_len, s5_g, s5_h, S5_T)

        merged = _merge(attn, sgu, y5, s5_glu_w[l], s5_glu_b[l], out_norm[l], tm)
        xs = _matmul_res(merged, w_out[l].astype(BF16), xs, mod, 2, lat_blocks, tm, _tile(d, 1024, 128))

        hf = _norm_mod(xs, norm_ffn[l], mod, 3, lat_blocks, tm)
        ids_t, gates_t = _peer_select(hf, peer_wq[l].astype(BF16), peer_k1[l], peer_k2[l], tm)
        tab = jnp.concatenate([peer_u[l], peer_v[l]], axis=1)
        xs = _peer_ffn(xs, ids_t.T, gates_t.T, norm_ffn[l], mod, 3, tab, grp_blocks, tg)

    out = _final_norm(xs, final_norm, n_lat, tm)
    return out.reshape(n_batch, seq, d)
```

```python
import functools
import math

import numpy as np
import jax
import jax.numpy as jnp
from jax import lax
from jax.experimental import pallas as pl
from jax.experimental.pallas import tpu as pltpu

F32 = jnp.float32
BF16 = jnp.bfloat16
EPS = 1e-6
GRID_W = 64
PEER_TOPK = 16
S5_T = 64
NEG = -1e30
HI = lax.Precision.HIGHEST
VMEM_LIMIT = 56 * 1024 * 1024


def _cp(sem, vmem=VMEM_LIMIT):
    return pltpu.CompilerParams(dimension_semantics=sem, vmem_limit_bytes=vmem)


def _tile(n, cap, unit):
    return max(t for t in range(unit, min(n, cap) + 1, unit) if n % t == 0)


def _gelu(x):
    return 0.5 * x * (1.0 + lax.erf(x * (1.0 / math.sqrt(2.0))))


def _dot_t(a, b, precision=None):
    return lax.dot_general(a, b, (((1,), (1,)), ((), ())), precision=precision,
                           preferred_element_type=F32)


def _dot(a, b, precision=None):
    return lax.dot_general(a, b, (((1,), (0,)), ((), ())), precision=precision,
                           preferred_element_type=F32)


def _mod_row(i, n_lat_blocks, blocks_per_batch, n_batch):
    return jnp.where(i < n_lat_blocks, i // blocks_per_batch, n_batch)


def _ada_kernel(c_ref, w_ref, b_ref, o_ref):
    c = c_ref[...]
    s = (c * jax.nn.sigmoid(c)).astype(BF16)
    o_ref[0] = _dot(s, w_ref[0].astype(BF16)) + b_ref[0]


def _ada_mod(cond, ada_w, ada_b, tn=1024):
    depth, d, nm = ada_w.shape
    rows = cond.shape[0]
    return pl.pallas_call(
        _ada_kernel,
        grid=(depth, nm // tn),
        in_specs=[pl.BlockSpec((rows, d), lambda l, j: (0, 0)),
                  pl.BlockSpec((1, d, tn), lambda l, j: (l, 0, j)),
                  pl.BlockSpec((1, 1, tn), lambda l, j: (l, 0, j))],
        out_specs=pl.BlockSpec((1, rows, tn), lambda l, j: (l, 0, j)),
        out_shape=jax.ShapeDtypeStruct((depth, rows, nm), F32),
        compiler_params=_cp(("arbitrary", "arbitrary")),
        name="ada_mod",
    )(cond, ada_w, ada_b.reshape(depth, 1, nm))


def _norm_mod_kernel(x_ref, g_ref, m_ref, o_ref, *, row):
    x = x_ref[...]
    m = m_ref[0]
    y = x * lax.rsqrt(jnp.mean(x * x, axis=-1, keepdims=True) + EPS) * g_ref[...]
    o_ref[...] = (y * (1.0 + m[row + 1:row + 2]) + m[row:row + 1]).astype(o_ref.dtype)


def _norm_mod(x, g, mod, row, lat_blocks, tm):
    n, d = x.shape
    nlb, bpb, nb = lat_blocks
    return pl.pallas_call(
        functools.partial(_norm_mod_kernel, row=row),
        grid=(n // tm,),
        in_specs=[pl.BlockSpec((tm, d), lambda i: (i, 0)),
                  pl.BlockSpec((1, d), lambda i: (0, 0)),
                  pl.BlockSpec((1, mod.shape[1], d), lambda i: (_mod_row(i, nlb, bpb, nb), 0, 0))],
        out_specs=pl.BlockSpec((tm, d), lambda i: (i, 0)),
        out_shape=jax.ShapeDtypeStruct((n, d), BF16),
        compiler_params=_cp(("arbitrary",)),
        name="norm_mod",
    )(x, g.reshape(1, d), mod)


def _mm_kernel(a_ref, w_ref, o_ref):
    o_ref[...] = _dot(a_ref[...], w_ref[...])


def _mm_res_kernel(a_ref, w_ref, r_ref, m_ref, o_ref, *, row):
    o_ref[...] = r_ref[...] + m_ref[0][row:row + 1] * _dot(a_ref[...], w_ref[...])


def _matmul(a, w, tm, tn):
    m, k = a.shape
    nn = w.shape[1]
    return pl.pallas_call(
        _mm_kernel,
        grid=(nn // tn, m // tm),
        in_specs=[pl.BlockSpec((tm, k), lambda j, i: (i, 0)),
                  pl.BlockSpec((k, tn), lambda j, i: (0, j))],
        out_specs=pl.BlockSpec((tm, tn), lambda j, i: (i, j)),
        out_shape=jax.ShapeDtypeStruct((m, nn), F32),
        compiler_params=_cp(("arbitrary", "arbitrary")),
        name="matmul",
    )(a, w)


def _matmul_res(a, w, res, mod, row, lat_blocks, tm, tn):
    m, k = a.shape
    nn = w.shape[1]
    nlb, bpb, nb = lat_blocks
    return pl.pallas_call(
        functools.partial(_mm_res_kernel, row=row),
        grid=(nn // tn, m // tm),
        in_specs=[pl.BlockSpec((tm, k), lambda j, i: (i, 0)),
                  pl.BlockSpec((k, tn), lambda j, i: (0, j)),
                  pl.BlockSpec((tm, tn), lambda j, i: (i, j)),
                  pl.BlockSpec((1, mod.shape[1], tn), lambda j, i: (_mod_row(i, nlb, bpb, nb), 0, j))],
        out_specs=pl.BlockSpec((tm, tn), lambda j, i: (i, j)),
        out_shape=jax.ShapeDtypeStruct((m, nn), F32),
        compiler_params=_cp(("arbitrary", "arbitrary")),
        name="matmul_res",
    )(a, w, res, mod)


def _na_tables(rows, width, kr, wc, rb_rows):
    band = rb_rows + kr
    nrb = rows // rb_rows
    masked = 2 * kr - 1
    ro = np.full((nrb, rb_rows, band), masked, dtype=np.int32)
    bstart = np.zeros((nrb,), dtype=np.int32)
    for rb in range(nrb):
        bs = int(np.clip(rb_rows * rb - kr // 2, 0, rows - band))
        bstart[rb] = bs
        for dlt in range(rb_rows):
            r = rb_rows * rb + dlt
            rs = int(np.clip(r - kr // 2, 0, rows - kr))
            for j in range(kr):
                ro[rb, dlt, rs + j - bs] = rs + j - r + (kr - 1)
    types, inv = np.unique(ro.reshape(nrb, -1), axis=0, return_inverse=True)
    types = types.reshape(-1, rb_rows, band)
    col = np.arange(width)
    cs = np.clip(col - wc // 2, 0, width - wc)
    kc = np.arange(width)
    valid = (kc[None, :] >= cs[:, None]) & (kc[None, :] < cs[:, None] + wc)
    co = np.clip(kc[None, :] - col[:, None] + (wc - 1), 0, 2 * wc - 2)
    return types, inv.reshape(-1).astype(np.int32), bstart, co, valid


def _na_bias(rpb, types, co, valid):
    h = rpb.shape[0]
    w = co.shape[0]
    tb = jnp.where(valid[None, None], rpb[:, :, co], NEG)
    tb = jnp.concatenate([tb, jnp.full((h, 1, w, w), NEG, F32)], axis=1)
    dense = tb[:, types]
    nt, rbr, band = types.shape
    dense = jnp.transpose(dense, (1, 0, 2, 4, 3, 5))
    return dense.reshape(nt, h, rbr * w, band * w)


def _na_kernel(type_ref, bstart_ref, q_ref, k_ref, v_ref, kc_ref, vc_ref, bias_ref, o_ref, *,
               band_tokens, width, scale):
    rb = pl.program_id(2)
    start = pl.multiple_of(bstart_ref[rb] * width, width)
    q = q_ref[...].astype(BF16)
    kb = k_ref[pl.ds(start, band_tokens), :].astype(BF16)
    vb = v_ref[pl.ds(start, band_tokens), :].astype(BF16)
    s_loc = _dot_t(q, kb) * scale + bias_ref[0, 0]
    s_ctx = _dot_t(q, kc_ref[...].astype(BF16)) * scale
    m = jnp.maximum(jnp.max(s_loc, axis=-1, keepdims=True), jnp.max(s_ctx, axis=-1, keepdims=True))
    p_loc = jnp.exp(s_loc - m)
    p_ctx = jnp.exp(s_ctx - m)
    den = jnp.sum(p_loc, axis=-1, keepdims=True) + jnp.sum(p_ctx, axis=-1, keepdims=True)
    o = _dot(p_loc.astype(BF16), vb) + _dot(p_ctx.astype(BF16), vc_ref[...].astype(BF16))
    o_ref[...] = o / den


def _na_attention(proj, bias, rb_type, bstart, n_batch, seq, ctx_len, heads, dh, kr, rb_rows):
    width = GRID_W
    nrb = seq // (rb_rows * width)
    qt = rb_rows * width
    band_tokens = (rb_rows + kr) * width
    ctx_blk0 = n_batch * seq // ctx_len
    grid_spec = pltpu.PrefetchScalarGridSpec(
        num_scalar_prefetch=2,
        grid=(n_batch, heads, nrb),
        in_specs=[
            pl.BlockSpec((qt, dh), lambda b, h, r, t, s: (b * nrb + r, h)),
            pl.BlockSpec((seq, dh), lambda b, h, r, t, s: (b, heads + h)),
            pl.BlockSpec((seq, dh), lambda b, h, r, t, s: (b, 2 * heads + h)),
            pl.BlockSpec((ctx_len, dh), lambda b, h, r, t, s: (ctx_blk0 + b, heads + h)),
            pl.BlockSpec((ctx_len, dh), lambda b, h, r, t, s: (ctx_blk0 + b, 2 * heads + h)),
            pl.BlockSpec((1, 1, qt, band_tokens), lambda b, h, r, t, s: (t[r], h, 0, 0)),
        ],
        out_specs=pl.BlockSpec((qt, dh), lambda b, h, r, t, s: (b * nrb + r, h)),
    )
    return pl.pallas_call(
        functools.partial(_na_kernel, band_tokens=band_tokens, width=width, scale=dh ** -0.5),
        grid_spec=grid_spec,
        out_shape=jax.ShapeDtypeStruct((n_batch * seq, heads * dh), F32),
        compiler_params=_cp(("arbitrary", "arbitrary", "arbitrary")),
        name="na_attention",
    )(rb_type, bstart, proj, proj, proj, proj, proj, bias)


def _ctx_attn_kernel(q_ref, k_ref, v_ref, o_ref, *, scale):
    s = _dot_t(q_ref[...].astype(BF16), k_ref[...].astype(BF16)) * scale
    m = jnp.max(s, axis=-1, keepdims=True)
    p = jnp.exp(s - m)
    den = jnp.sum(p, axis=-1, keepdims=True)
    o_ref[...] = _dot(p.astype(BF16), v_ref[...].astype(BF16)) / den


def _ctx_attention(proj, n_batch, seq, ctx_len, heads, dh):
    ctx_blk0 = n_batch * seq // ctx_len
    return pl.pallas_call(
        functools.partial(_ctx_attn_kernel, scale=dh ** -0.5),
        grid=(n_batch, heads),
        in_specs=[pl.BlockSpec((ctx_len, dh), lambda b, h: (ctx_blk0 + b, h)),
                  pl.BlockSpec((ctx_len, dh), lambda b, h: (ctx_blk0 + b, heads + h)),
                  pl.BlockSpec((ctx_len, dh), lambda b, h: (ctx_blk0 + b, 2 * heads + h))],
        out_specs=pl.BlockSpec((ctx_len, dh), lambda b, h: (b, h)),
        out_shape=jax.ShapeDtypeStruct((n_batch * ctx_len, heads * dh), F32),
        compiler_params=_cp(("arbitrary", "arbitrary")),
        name="ctx_attention",
    )(proj, proj, proj)


def _sgu_kernel(su_ref, sv_ref, w_ref, b_ref, g_ref, o_ref, *, chunk, heads, hd):
    u = _gelu(su_ref[...])
    gv = _gelu(sv_ref[...])
    v = (gv * lax.rsqrt(jnp.mean(gv * gv, axis=-1, keepdims=True) + EPS) * g_ref[...]).astype(BF16)
    for n in range(u.shape[0] // chunk):
        rs = slice(n * chunk, (n + 1) * chunk)
        for h in range(heads):
            cs = slice(h * hd, (h + 1) * hd)
            mixed = _dot(w_ref[h], v[rs, cs]) + b_ref[h]
            o_ref[rs, cs] = u[rs, cs] * mixed


def _sgu(proj, sgu_w, sgu_b, sgu_norm, col_blk, tm):
    n = proj.shape[0]
    heads, chunk, _ = sgu_w.shape
    sw = sgu_norm.shape[-1]
    hd = sw // heads
    bias = jnp.broadcast_to(sgu_b[:, :, None], (heads, chunk, hd))
    return pl.pallas_call(
        functools.partial(_sgu_kernel, chunk=chunk, heads=heads, hd=hd),
        grid=(n // tm,),
        in_specs=[pl.BlockSpec((tm, sw), lambda i: (i, col_blk)),
                  pl.BlockSpec((tm, sw), lambda i: (i, col_blk + 1)),
                  pl.BlockSpec((heads, chunk, chunk), lambda i: (0, 0, 0)),
                  pl.BlockSpec((heads, chunk, hd), lambda i: (0, 0, 0)),
                  pl.BlockSpec((1, sw), lambda i: (0, 0))],
        out_specs=pl.BlockSpec((tm, sw), lambda i: (i, 0)),
        out_shape=jax.ShapeDtypeStruct((n, sw), F32),
        compiler_params=_cp(("arbitrary",)),
        name="sgu",
    )(proj, proj, sgu_w.astype(BF16), bias, sgu_norm.reshape(1, sw))


def _s5_prep_kernel(are_ref, aim_ref, ldt_ref, btr_ref, bti_ref, cr_ref, ci_ref,
                    mt_ref, bp_ref, cp_ref, lam_ref, *, t_len, n_h, n_p):
    rev = pl.program_id(1) == 1
    sgn = jnp.where(rev, -1.0, 1.0)
    are = jnp.minimum(are_ref[0, 0, 0], -1e-4)
    aim = aim_ref[0, 0, 0]
    dt = jnp.exp(ldt_ref[0, 0, 0])
    lr = are * dt
    li = aim * dt
    first = lax.broadcasted_iota(jnp.int32, (1, 2 * n_p), 1) < n_p

    def power(n):
        mag = jnp.exp(n * lr)
        ang = n * li
        return mag * jnp.cos(ang), mag * jnp.sin(ang)

    lbr = jnp.exp(lr) * jnp.cos(li)
    lbi = jnp.exp(lr) * jnp.sin(li)
    den = are * are + aim * aim
    nr = lbr - 1.0
    kr_ = (nr * are + lbi * aim) / den
    ki_ = (lbi * are - nr * aim) / den
    btr = btr_ref[0, 0, 0]
    bti = bti_ref[0, 0, 0]
    bbr = kr_ * btr - ki_ * bti
    bbi = kr_ * bti + ki_ * btr
    ccr = cr_ref[0, 0, 0]
    cci = ci_ref[0, 0, 0]
    b_a = jnp.where(first, bbr, -bbi)
    b_b = jnp.where(first, -bbi, -bbr)
    c_a = jnp.where(first, ccr, cci)
    c_b = jnp.where(first, -cci, ccr)
    b_ri_a = jnp.where(first, bbr, bbi)
    b_ri_b = jnp.where(first, -bbi, bbr)
    c_n_a = jnp.where(first, ccr, -cci)
    c_n_b = jnp.where(first, -cci, -ccr)

    th = t_len * n_h
    tt = (lax.broadcasted_iota(jnp.int32, (th, 1), 0) // n_h).astype(F32)
    half = 0.5 * t_len

    def expand(pw, xa, xb):
        pr, pi_ = pw
        return (pr * jnp.concatenate([xa] * t_len, axis=0)
                + pi_ * jnp.concatenate([xb] * t_len, axis=0))

    a_cat = expand(power(sgn * (half - tt)), b_a, b_b)
    c_cat = expand(power(sgn * (tt - half)), c_a, c_b)
    mt = _dot_t(a_cat, c_cat, precision=HI)
    rs = lax.broadcasted_iota(jnp.int32, (th, th), 0) // n_h
    ct = lax.broadcasted_iota(jnp.int32, (th, th), 1) // n_h
    keep = (ct - rs) * jnp.where(rev, -1, 1) >= 0
    mt_ref[0, 0, 0] = jnp.where(keep, mt, 0.0).astype(mt_ref.dtype)

    n_out = jnp.where(rev, tt, (t_len - 1.0) - tt)
    bp_ref[0, 0, 0] = expand(power(n_out), b_ri_a, b_ri_b)
    n_in = jnp.where(rev, t_len - tt, tt + 1.0)
    cp_ref[0, 0, 0] = expand(power(n_in), c_n_a, c_n_b)
    ltr, lti = power(jnp.full((1, 1), float(t_len), F32))
    lam_ref[0, 0, 0, 0:1, :] = ltr
    lam_ref[0, 0, 0, 1:2, :] = jnp.where(first, -lti, lti)


def _s5_prep(a_re, a_im, log_dt, b_re, b_im, c_re, c_im, t_len):
    depth, _, g, p, h = b_re.shape
    dup = lambda t: jnp.concatenate([t, t], axis=-1)
    are = dup(a_re).reshape(depth, 2, g, 1, 2 * p)
    aim = dup(a_im).reshape(depth, 2, g, 1, 2 * p)
    ldt = log_dt.reshape(depth, 2, g, 1, 1)
    btr = dup(jnp.swapaxes(b_re, -1, -2))
    bti = dup(jnp.swapaxes(b_im, -1, -2))
    ccr = dup(c_re)
    cci = dup(c_im)
    th = t_len * h
    row = lambda r, c: pl.BlockSpec((1, 1, 1, r, c), lambda l, d, k: (l, d, k, 0, 0))
    return pl.pallas_call(
        functools.partial(_s5_prep_kernel, t_len=t_len, n_h=h, n_p=p),
        grid=(depth, 2, g),
        in_specs=[row(1, 2 * p), row(1, 2 * p), row(1, 1), row(h, 2 * p), row(h, 2 * p),
                  row(h, 2 * p), row(h, 2 * p)],
        out_specs=[row(th, th), row(th, 2 * p), row(th, 2 * p), row(2, 2 * p)],
        out_shape=[jax.ShapeDtypeStruct((depth, 2, g, th, th), BF16),
                   jax.ShapeDtypeStruct((depth, 2, g, th, 2 * p), F32),
                   jax.ShapeDtypeStruct((depth, 2, g, th, 2 * p), F32),
                   jax.ShapeDtypeStruct((depth, 2, g, 2, 2 * p), F32)],
        compiler_params=_cp(("arbitrary", "arbitrary", "arbitrary")),
        name="s5_prep",
    )(are, aim, ldt, btr, bti, ccr, cci)


def _s5_kernel(u_ref, d_ref, mt_ref, bp_ref, cp_ref, lam_ref, y_ref, hin_ref, *, n_ctx_chunks, n_p):
    u = u_ref[0]
    ub = u.astype(BF16)
    n_chunks = hin_ref.shape[0]
    y = u * d_ref[0]
    for d in range(2):
        y = y + _dot(ub, mt_ref[d, 0])
        hloc = _dot(u, bp_ref[d, 0], precision=HI).reshape(n_chunks, 8, 2 * n_p)
        lam_a = lam_ref[d, 0, 0:1, :]
        lam_b = lam_ref[d, 0, 1:2, :]
        if d == 0:
            order = list(range(n_chunks))
        else:
            order = list(range(n_ctx_chunks - 1, -1, -1)) + list(range(n_chunks - 1, n_ctx_chunks - 1, -1))
        h = jnp.zeros((8, 2 * n_p), F32)
        for c in order:
            hin_ref[c] = h
            h = h * lam_a + pltpu.roll(h, n_p, axis=1) * lam_b + hloc[c]
        hin = hin_ref[...].reshape(n_chunks * 8, 2 * n_p)
        y = y + _dot_t(hin, cp_ref[d, 0], precision=HI)
    y_ref[0] = y


def _s5_scan(u_g, d_tile, mt, bp, cp, lam, n_ctx_chunks):
    g, r, th = u_g.shape
    p2 = bp.shape[-1]
    n_chunks = r // 8
    return pl.pallas_call(
        functools.partial(_s5_kernel, n_ctx_chunks=n_ctx_chunks, n_p=p2 // 2),
        grid=(g,),
        in_specs=[pl.BlockSpec((1, r, th), lambda k: (k, 0, 0)),
                  pl.BlockSpec((1, 1, th), lambda k: (k, 0, 0)),
                  pl.BlockSpec((2, 1, th, th), lambda k: (0, k, 0, 0)),
                  pl.BlockSpec((2, 1, th, p2), lambda k: (0, k, 0, 0)),
                  pl.BlockSpec((2, 1, th, p2), lambda k: (0, k, 0, 0)),
                  pl.BlockSpec((2, 1, 2, p2), lambda k: (0, k, 0, 0))],
        out_specs=pl.BlockSpec((1, r, th), lambda k: (k, 0, 0)),
        out_shape=jax.ShapeDtypeStruct((g, r, th), F32),
        scratch_shapes=[pltpu.VMEM((n_chunks, 8, p2), F32)],
        compiler_params=_cp(("arbitrary",)),
        name="s5_scan",
    )(u_g, d_tile, mt, bp, cp, lam)


def _s5_to_groups(s_in, n_batch, seq, ctx_len, g, h, t_len):
    n_lat = n_batch * seq
    lat = s_in[:n_lat].reshape(n_batch, seq // t_len, t_len, g, h)
    cx = s_in[n_lat:].reshape(n_batch, ctx_len // t_len, t_len, g, h)
    al = jnp.concatenate([cx, lat], axis=1)
    al = jnp.transpose(al, (3, 1, 0, 2, 4))
    al = jnp.pad(al, ((0, 0), (0, 0), (0, 8 - n_batch), (0, 0), (0, 0)))
    return al.reshape(g, al.shape[1] * 8, t_len * h)


def _s5_from_groups(y_g, n_batch, seq, ctx_len, g, h, t_len):
    n_chunks = y_g.shape[1] // 8
    y = y_g.reshape(g, n_chunks, 8, t_len, h)[:, :, :n_batch]
    y = jnp.transpose(y, (2, 1, 3, 0, 4))
    ncc = ctx_len // t_len
    cx = y[:, :ncc].reshape(n_batch * ctx_len, g * h)
    lat = y[:, ncc:].reshape(n_batch * seq, g * h)
    return jnp.concatenate([lat, cx], axis=0)


def _merge_kernel(a_ref, s_ref, y_ref, gw_ref, gb_ref, n_ref, o_ref, *, wa, ws):
    def rms(t, g):
        return t * lax.rsqrt(jnp.mean(t * t, axis=-1, keepdims=True) + EPS) * g

    y = _gelu(y_ref[...])
    z = y * jax.nn.sigmoid(_dot(y.astype(BF16), gw_ref[...]) + gb_ref[...])
    o_ref[:, :wa] = rms(a_ref[...], n_ref[:, :wa]).astype(o_ref.dtype)
    o_ref[:, wa:wa + ws] = rms(s_ref[...], n_ref[:, wa:wa + ws]).astype(o_ref.dtype)
    o_ref[:, wa + ws:] = rms(z, n_ref[:, wa + ws:]).astype(o_ref.dtype)


def _merge(attn, sgu, y5, glu_w, glu_b, out_norm, tm):
    n, wa = attn.shape
    ws = sgu.shape[1]
    w5 = y5.shape[1]
    wt = wa + ws + w5
    return pl.pallas_call(
        functools.partial(_merge_kernel, wa=wa, ws=ws),
        grid=(n // tm,),
        in_specs=[pl.BlockSpec((tm, wa), lambda i: (i, 0)),
                  pl.BlockSpec((tm, ws), lambda i: (i, 0)),
                  pl.BlockSpec((tm, w5), lambda i: (i, 0)),
                  pl.BlockSpec((w5, w5), lambda i: (0, 0)),
                  pl.BlockSpec((1, w5), lambda i: (0, 0)),
                  pl.BlockSpec((1, wt), lambda i: (0, 0))],
        out_specs=pl.BlockSpec((tm, wt), lambda i: (i, 0)),
        out_shape=jax.ShapeDtypeStruct((n, wt), BF16),
        compiler_params=_cp(("arbitrary",)),
        name="merge_norm",
    )(attn, sgu, y5, glu_w.astype(BF16), glu_b.reshape(1, w5), out_norm.reshape(1, wt))


def _topk_cols(s, k, payload=None):
    n, t = s.shape
    iota = lax.broadcasted_iota(jnp.int32, (n, t), 0).astype(F32)
    krow = lax.broadcasted_iota(jnp.int32, (k, t), 0)
    vals = jnp.zeros((k, t), F32)
    outs = jnp.zeros((k, t), F32)
    for it in range(k):
        m = jnp.max(s, axis=0, keepdims=True)
        pos = jnp.min(jnp.where(s == m, iota, float(n)), axis=0, keepdims=True)
        hit = iota == pos
        if payload is None:
            sel = pos
        else:
            sel = jnp.max(jnp.where(hit, payload, -1.0), axis=0, keepdims=True)
        vals = jnp.where(krow == it, m, vals)
        outs = jnp.where(krow == it, sel, outs)
        s = jnp.where(hit, -jnp.inf, s)
    return vals, outs


def _staircase_candidates(v1, i1, v2, i2, k, nkeys):
    s = math.isqrt(k)
    t = v1.shape[1]
    r = lax.broadcasted_iota(jnp.int32, (8, t), 0)
    sc, ex = [], []

    def emit(vs, es, lo, hi):
        ok = (r >= lo) & (r < hi)
        sc.append(jnp.where(ok, vs, -jnp.inf))
        ex.append(es)

    for b in range(s):
        na = k // (b + 1)
        for a0 in range(0, na, 8):
            emit(v1[a0:a0 + 8] + v2[b:b + 1], i1[a0:a0 + 8] * nkeys + i2[b:b + 1], 0, na - a0)
    for a in range(s):
        nb = k // (a + 1)
        for b0 in range((s // 8) * 8, nb, 8):
            if nb - b0 > max(s - b0, 0):
                emit(v1[a:a + 1] + v2[b0:b0 + 8], i1[a:a + 1] * nkeys + i2[b0:b0 + 8], s - b0, nb - b0)
    return jnp.concatenate(sc, axis=0), jnp.concatenate(ex, axis=0)


def _peer_sel_kernel(h_ref, wq_ref, k1_ref, k2_ref, ids_ref, gate_ref, q_scr, *, heads, qdim, nkeys, topk, tl):
    q_scr[...] = _dot(h_ref[...], wq_ref[...])
    half = qdim // 2

    def head_body(hh, carry):
        for th in range(h_ref.shape[0] // tl):
            q1 = q_scr[pl.ds(th * tl, tl), pl.ds(pl.multiple_of(hh * qdim, qdim), half)]
            q2 = q_scr[pl.ds(th * tl, tl), pl.ds(pl.multiple_of(hh * qdim + half, half), half)]
            s1 = _dot_t(k1_ref[hh], q1, precision=HI)
            s2 = _dot_t(k2_ref[hh], q2, precision=HI)
            v1, i1 = _topk_cols(s1, topk)
            v2, i2 = _topk_cols(s2, topk)
            cand_s, cand_e = _staircase_candidates(v1, i1, v2, i2, topk, float(nkeys))
            best, expert = _topk_cols(cand_s, topk, payload=cand_e)
            e = jnp.exp(best - best[0:1])
            gate = e / jnp.sum(e, axis=0, keepdims=True)
            rows = pl.ds(pl.multiple_of(hh * topk, topk), topk)
            ids_ref[rows, pl.ds(th * tl, tl)] = expert.astype(jnp.int32)
            gate_ref[rows, pl.ds(th * tl, tl)] = gate
        return carry

    lax.fori_loop(0, heads, head_body, 0)


def _peer_select(h, wq, k1, k2, tm, tl=128):
    n, d = h.shape
    heads, nkeys, half = k1.shape
    qdim = 2 * half
    nsel = heads * PEER_TOPK
    return pl.pallas_call(
        functools.partial(_peer_sel_kernel, heads=heads, qdim=qdim, nkeys=nkeys, topk=PEER_TOPK, tl=tl),
        grid=(n // tm,),
        in_specs=[pl.BlockSpec((tm, d), lambda i: (i, 0)),
                  pl.BlockSpec((d, heads * qdim), lambda i: (0, 0)),
                  pl.BlockSpec((heads, nkeys, half), lambda i: (0, 0, 0)),
                  pl.BlockSpec((heads, nkeys, half), lambda i: (0, 0, 0))],
        out_specs=[pl.BlockSpec((nsel, tm), lambda i: (0, i)),
                   pl.BlockSpec((nsel, tm), lambda i: (0, i))],
        out_shape=[jax.ShapeDtypeStruct((nsel, n), jnp.int32),
                   jax.ShapeDtypeStruct((nsel, n), F32)],
        scratch_shapes=[pltpu.VMEM((tm, heads * qdim), F32)],
        compiler_params=_cp(("arbitrary",)),
        name="peer_select",
    )(h, wq, k1, k2)


def _pack_tables(u, v):
    ub = lax.bitcast_convert_type(u.astype(BF16), jnp.uint16).astype(jnp.uint32)
    vb = lax.bitcast_convert_type(v.astype(BF16), jnp.uint16).astype(jnp.uint32)
    return (ub << 16) | vb


def _peer_ffn_kernel(ids_ref, idn_ref, g_ref, x_ref, n_ref, m_ref, tab_ref, o_ref, buf0_ref, buf1_ref, sem_ref,
                     *, tg, nsel, row):
    i = pl.program_id(0)
    nsteps = pl.num_programs(0)
    m = m_ref[0]
    eye = (lax.broadcasted_iota(jnp.int32, (nsel, nsel), 0)
           == lax.broadcasted_iota(jnp.int32, (nsel, nsel), 1))

    def row_copy(e, buf, slot, k, j):
        return pltpu.make_async_copy(tab_ref.at[pl.ds(e, 1)], buf.at[pl.ds(k * nsel + j, 1)],
                                     sem_ref.at[slot, k])

    def token_wait(buf, slot, k):
        pltpu.make_async_copy(tab_ref.at[pl.ds(0, nsel)], buf.at[pl.ds(k * nsel, nsel)],
                              sem_ref.at[slot, k]).wait()

    @pl.when(i == 0)
    def _():
        for k in range(tg):
            def body(j, carry, k=k):
                row_copy(ids_ref[k, j], buf0_ref, 0, k, j).start()
                return carry
            lax.fori_loop(0, nsel, body, 0)

    def step(cur, nxt, cs, ns):
        for k in range(tg):
            token_wait(cur, cs, k)
            for j in range(nsel):
                row_copy(idn_ref[k, j], nxt, ns, k, j).start()
            xk = x_ref[k:k + 1, :]
            hk = xk * lax.rsqrt(jnp.mean(xk * xk, axis=-1, keepdims=True) + EPS) * n_ref[...]
            hk = hk * (1.0 + m[row + 1:row + 2]) + m[row:row + 1]
            w = cur[k * nsel:(k + 1) * nsel, :]
            uf = lax.bitcast_convert_type(w & jnp.uint32(0xFFFF0000), F32)
            act = _gelu(jnp.sum(uf * hk, axis=1, keepdims=True))
            gcol = jnp.sum(jnp.where(eye, g_ref[k:k + 1, :], 0.0), axis=1, keepdims=True)
            vf = lax.bitcast_convert_type(w << 16, F32)
            out = jnp.sum(vf * (gcol * act), axis=0, keepdims=True)
            o_ref[k:k + 1, :] = xk + m[row + 2:row + 3] * out

        @pl.when(i == nsteps - 1)
        def _():
            for k in range(tg):
                token_wait(nxt, ns, k)

    @pl.when(i % 2 == 0)
    def _():
        step(buf0_ref, buf1_ref, 0, 1)

    @pl.when(i % 2 == 1)
    def _():
        step(buf1_ref, buf0_ref, 1, 0)


def _peer_ffn(x, ids, gates, norm_g, mod, row, tab, lat_blocks, tg=8):
    n, d = x.shape
    nsel = ids.shape[1]
    nlb, bpb, nb = lat_blocks
    steps = n // tg
    return pl.pallas_call(
        functools.partial(_peer_ffn_kernel, tg=tg, nsel=nsel, row=row),
        grid=(steps,),
        in_specs=[pl.BlockSpec((tg, nsel), lambda i: (i, 0), memory_space=pltpu.SMEM),
                  pl.BlockSpec((tg, nsel), lambda i: (jnp.minimum(i + 1, steps - 1), 0),
                               memory_space=pltpu.SMEM),
                  pl.BlockSpec((tg, nsel), lambda i: (i, 0)),
                  pl.BlockSpec((tg, d), lambda i: (i, 0)),
                  pl.BlockSpec((1, d), lambda i: (0, 0)),
                  pl.BlockSpec((1, mod.shape[1], d), lambda i: (_mod_row(i, nlb, bpb, nb), 0, 0)),
                  pl.BlockSpec(memory_space=pl.ANY)],
        out_specs=pl.BlockSpec((tg, d), lambda i: (i, 0)),
        out_shape=jax.ShapeDtypeStruct((n, d), F32),
        scratch_shapes=[pltpu.VMEM((tg * nsel, d), jnp.uint32),
                        pltpu.VMEM((tg * nsel, d), jnp.uint32),
                        pltpu.SemaphoreType.DMA((2, tg))],
        compiler_params=_cp(("arbitrary",)),
        name="peer_ffn",
    )(ids, ids, gates, x, norm_g.reshape(1, d), mod, tab)


def _final_norm_kernel(x_ref, g_ref, o_ref):
    x = x_ref[...]
    o_ref[...] = x * lax.rsqrt(jnp.mean(x * x, axis=-1, keepdims=True) + EPS) * g_ref[...]


def _final_norm(x, g, n_rows, tm):
    d = x.shape[1]
    return pl.pallas_call(
        _final_norm_kernel,
        grid=(n_rows // tm,),
        in_specs=[pl.BlockSpec((tm, d), lambda i: (i, 0)),
                  pl.BlockSpec((1, d), lambda i: (0, 0))],
        out_specs=pl.BlockSpec((tm, d), lambda i: (i, 0)),
        out_shape=jax.ShapeDtypeStruct((n_rows, d), F32),
        compiler_params=_cp(("arbitrary",)),
        name="final_norm",
    )(x, g.reshape(1, d))


def kernel(x, c, ctx, c_ctx, ada_w, ada_b, norm_mix, norm_ffn, w_in, w_out, out_norm, na_rpb, sgu_w, sgu_b,
           sgu_norm, s5_a_re, s5_a_im, s5_log_dt, s5_b_re, s5_b_im, s5_c_re, s5_c_im, s5_d, s5_glu_w,
           s5_glu_b, peer_wq, peer_k1, peer_k2, peer_u, peer_v, final_norm):
    n_batch, seq, d = x.shape
    ctx_len = ctx.shape[1]
    depth = ada_w.shape[0]
    n_mod = ada_w.shape[2] // d
    in_w = w_in.shape[2]
    sgu_width = sgu_norm.shape[-1]
    s5_width = s5_d.shape[-1]
    na_width = (in_w - 2 * sgu_width - s5_width) // 3
    na_heads = na_rpb.shape[1]
    dh = na_width // na_heads
    kr_full = (na_rpb.shape[2] + 1) // 2
    wc = (na_rpb.shape[3] + 1) // 2
    rows = seq // GRID_W
    kr = min(kr_full, rows)
    _, _, s5_g, s5_p, s5_h = s5_b_re.shape
    n_lat = n_batch * seq
    n_tok = n_lat + n_batch * ctx_len

    tm = 256
    lat_blocks = (n_lat // tm, seq // tm, n_batch)
    tg = 8
    grp_blocks = (n_lat // tg, seq // tg, n_batch)

    cond = jnp.concatenate([c, c_ctx[None], jnp.zeros((8 - n_batch - 1, d), F32)], axis=0)
    mods = _ada_mod(cond, ada_w, ada_b).reshape(depth, 8, n_mod, d)

    rb_rows = 8
    types, rb_type, bstart, co, valid = _na_tables(rows, GRID_W, kr, wc, rb_rows)
    types_ref = np.where(types == 2 * kr - 1, 2 * kr_full - 1, types + (kr_full - kr))
    s5_ops = _s5_prep(s5_a_re, s5_a_im, s5_log_dt, s5_b_re, s5_b_im, s5_c_re, s5_c_im, S5_T)

    xs = jnp.concatenate([x.reshape(n_lat, d), ctx.reshape(n_batch * ctx_len, d)], axis=0)
    for l in range(depth):
        mod = mods[l]
        h = _norm_mod(xs, norm_mix[l], mod, 0, lat_blocks, tm)
        proj = _matmul(h, w_in[l].astype(BF16), _tile(n_tok, 512, 8), _tile(in_w, 1152, 128))

        bias = _na_bias(na_rpb[l], types_ref, co, valid)
        attn_lat = _na_attention(proj, bias, jnp.asarray(rb_type), jnp.asarray(bstart), n_batch, seq, ctx_len,
                                 na_heads, dh, kr, rb_rows)
        attn_ctx = _ctx_attention(proj, n_batch, seq, ctx_len, na_heads, dh)
        attn = jnp.concatenate([attn_lat, attn_ctx], axis=0)

        sgu = _sgu(proj, sgu_w[l], sgu_b[l], sgu_norm[l], 3 * na_width // sgu_width, tm)

        s_in = proj[:, in_w - s5_width:]
        u_g = _s5_to_groups(s_in, n_batch, seq, ctx_len, s5_g, s5_h, S5_T)
        d_tile = jnp.tile(s5_d[l].reshape(s5_g, 1, s5_h), (1, 1, S5_T))
        y_g = _s5_scan(u_g, d_tile, s5_ops[0][l], s5_ops[1][l], s5_ops[2][l], s5_ops[3][l], ctx_len // S5_T)
        y5 = _s5_from_groups(y_g, n_batch, seq, ctx_len, s5_g, s5_h, S5_T)

        merged = _merge(attn, sgu, y5, s5_glu_w[l], s5_glu_b[l], out_norm[l], tm)
        xs = _matmul_res(merged, w_out[l].astype(BF16), xs, mod, 2, lat_blocks, tm, _tile(d, 1024, 128))

        hf = _norm_mod(xs, norm_ffn[l], mod, 3, lat_blocks, tm)
        ids_t, gates_t = _peer_select(hf, peer_wq[l].astype(BF16), peer_k1[l], peer_k2[l], tm)
        tab = _pack_tables(peer_u[l], peer_v[l])
        xs = _peer_ffn(xs, ids_t.T, gates_t.T, norm_ffn[l], mod, 3, tab, grp_blocks, tg)

    out = _final_norm(xs, final_norm, n_lat, tm)
    return out.reshape(n_batch, seq, d)
```

```python
import functools
import math

import numpy as np
import jax
import jax.numpy as jnp
from jax import lax
from jax.experimental import pallas as pl
from jax.experimental.pallas import tpu as pltpu

F32 = jnp.float32
BF16 = jnp.bfloat16
EPS = 1e-6
GRID_W = 64
PEER_TOPK = 16
S5_T = 64
NEG = -1e30
HI = lax.Precision.HIGHEST
VMEM_LIMIT = 56 * 1024 * 1024


def _cp(sem, vmem=VMEM_LIMIT):
    return pltpu.CompilerParams(dimension_semantics=sem, vmem_limit_bytes=vmem)


def _tile(n, cap, unit):
    return max(t for t in range(unit, min(n, cap) + 1, unit) if n % t == 0)


def _gelu(x):
    return 0.5 * x * (1.0 + lax.erf(x * (1.0 / math.sqrt(2.0))))


def _dot_t(a, b, precision=None):
    return lax.dot_general(a, b, (((1,), (1,)), ((), ())), precision=precision,
                           preferred_element_type=F32)


def _dot(a, b, precision=None):
    return lax.dot_general(a, b, (((1,), (0,)), ((), ())), precision=precision,
                           preferred_element_type=F32)


def _bf16_pieces(x, n):
    out = []
    for _ in range(n):
        p = x.astype(BF16)
        out.append(p)
        x = x - p.astype(F32)
    return out


def _mod_row(i, n_lat_blocks, blocks_per_batch, n_batch):
    return jnp.where(i < n_lat_blocks, i // blocks_per_batch, n_batch)


def _ada_kernel(c_ref, w_ref, b_ref, o_ref):
    c = c_ref[...]
    s = (c * jax.nn.sigmoid(c)).astype(BF16)
    o_ref[0] = _dot(s, w_ref[0].astype(BF16)) + b_ref[0]


def _ada_mod(cond, ada_w, ada_b, tn=1024):
    depth, d, nm = ada_w.shape
    rows = cond.shape[0]
    return pl.pallas_call(
        _ada_kernel,
        grid=(depth, nm // tn),
        in_specs=[pl.BlockSpec((rows, d), lambda l, j: (0, 0)),
                  pl.BlockSpec((1, d, tn), lambda l, j: (l, 0, j)),
                  pl.BlockSpec((1, 1, tn), lambda l, j: (l, 0, j))],
        out_specs=pl.BlockSpec((1, rows, tn), lambda l, j: (l, 0, j)),
        out_shape=jax.ShapeDtypeStruct((depth, rows, nm), F32),
        compiler_params=_cp(("arbitrary", "arbitrary")),
        name="ada_mod",
    )(cond, ada_w, ada_b.reshape(depth, 1, nm))


def _norm_mod_kernel(x_ref, g_ref, m_ref, o_ref, *, row):
    x = x_ref[...]
    m = m_ref[0]
    y = x * lax.rsqrt(jnp.mean(x * x, axis=-1, keepdims=True) + EPS) * g_ref[...]
    o_ref[...] = (y * (1.0 + m[row + 1:row + 2]) + m[row:row + 1]).astype(o_ref.dtype)


def _norm_mod(x, g, mod, row, lat_blocks, tm):
    n, d = x.shape
    nlb, bpb, nb = lat_blocks
    return pl.pallas_call(
        functools.partial(_norm_mod_kernel, row=row),
        grid=(n // tm,),
        in_specs=[pl.BlockSpec((tm, d), lambda i: (i, 0)),
                  pl.BlockSpec((1, d), lambda i: (0, 0)),
                  pl.BlockSpec((1, mod.shape[1], d), lambda i: (_mod_row(i, nlb, bpb, nb), 0, 0))],
        out_specs=pl.BlockSpec((tm, d), lambda i: (i, 0)),
        out_shape=jax.ShapeDtypeStruct((n, d), BF16),
        compiler_params=_cp(("arbitrary",)),
        name="norm_mod",
    )(x, g.reshape(1, d), mod)


def _mm_kernel(a_ref, w_ref, o_ref):
    o_ref[...] = _dot(a_ref[...], w_ref[...])


def _mm_res_kernel(a_ref, w_ref, r_ref, m_ref, o_ref, *, row):
    o_ref[...] = r_ref[...] + m_ref[0][row:row + 1] * _dot(a_ref[...], w_ref[...])


def _matmul(a, w, tm, tn):
    m, k = a.shape
    nn = w.shape[1]
    return pl.pallas_call(
        _mm_kernel,
        grid=(nn // tn, m // tm),
        in_specs=[pl.BlockSpec((tm, k), lambda j, i: (i, 0)),
                  pl.BlockSpec((k, tn), lambda j, i: (0, j))],
        out_specs=pl.BlockSpec((tm, tn), lambda j, i: (i, j)),
        out_shape=jax.ShapeDtypeStruct((m, nn), F32),
        compiler_params=_cp(("arbitrary", "arbitrary")),
        name="matmul",
    )(a, w)


def _matmul_res(a, w, res, mod, row, lat_blocks, tm, tn):
    m, k = a.shape
    nn = w.shape[1]
    nlb, bpb, nb = lat_blocks
    return pl.pallas_call(
        functools.partial(_mm_res_kernel, row=row),
        grid=(nn // tn, m // tm),
        in_specs=[pl.BlockSpec((tm, k), lambda j, i: (i, 0)),
                  pl.BlockSpec((k, tn), lambda j, i: (0, j)),
                  pl.BlockSpec((tm, tn), lambda j, i: (i, j)),
                  pl.BlockSpec((1, mod.shape[1], tn), lambda j, i: (_mod_row(i, nlb, bpb, nb), 0, j))],
        out_specs=pl.BlockSpec((tm, tn), lambda j, i: (i, j)),
        out_shape=jax.ShapeDtypeStruct((m, nn), F32),
        compiler_params=_cp(("arbitrary", "arbitrary")),
        name="matmul_res",
    )(a, w, res, mod)


def _na_tables(rows, width, kr, wc, rb_rows):
    band = rb_rows + kr
    nrb = rows // rb_rows
    masked = 2 * kr - 1
    ro = np.full((nrb, rb_rows, band), masked, dtype=np.int32)
    bstart = np.zeros((nrb,), dtype=np.int32)
    for rb in range(nrb):
        bs = int(np.clip(rb_rows * rb - kr // 2, 0, rows - band))
        bstart[rb] = bs
        for dlt in range(rb_rows):
            r = rb_rows * rb + dlt
            rs = int(np.clip(r - kr // 2, 0, rows - kr))
            for j in range(kr):
                ro[rb, dlt, rs + j - bs] = rs + j - r + (kr - 1)
    types, inv = np.unique(ro.reshape(nrb, -1), axis=0, return_inverse=True)
    types = types.reshape(-1, rb_rows, band)
    col = np.arange(width)
    cs = np.clip(col - wc // 2, 0, width - wc)
    kc = np.arange(width)
    valid = (kc[None, :] >= cs[:, None]) & (kc[None, :] < cs[:, None] + wc)
    co = np.clip(kc[None, :] - col[:, None] + (wc - 1), 0, 2 * wc - 2)
    return types, inv.reshape(-1).astype(np.int32), bstart, co, valid


def _na_bias(rpb, types, co, valid):
    h = rpb.shape[0]
    w = co.shape[0]
    tb = jnp.where(valid[None, None], rpb[:, :, co], NEG)
    tb = jnp.concatenate([tb, jnp.full((h, 1, w, w), NEG, F32)], axis=1)
    dense = tb[:, types]
    nt, rbr, band = types.shape
    dense = jnp.transpose(dense, (1, 0, 2, 4, 3, 5))
    return dense.reshape(nt, h, rbr * w, band * w)


def _na_kernel(type_ref, bstart_ref, q_ref, k_ref, v_ref, kc_ref, vc_ref, bias_ref, o_ref, *,
               band_tokens, width, scale):
    rb = pl.program_id(2)
    start = pl.multiple_of(bstart_ref[rb] * width, width)
    q = q_ref[...].astype(BF16)
    kb = k_ref[pl.ds(start, band_tokens), :].astype(BF16)
    vb = v_ref[pl.ds(start, band_tokens), :].astype(BF16)
    s_loc = _dot_t(q, kb) * scale + bias_ref[0, 0]
    s_ctx = _dot_t(q, kc_ref[...].astype(BF16)) * scale
    m = jnp.maximum(jnp.max(s_loc, axis=-1, keepdims=True), jnp.max(s_ctx, axis=-1, keepdims=True))
    p_loc = jnp.exp(s_loc - m)
    p_ctx = jnp.exp(s_ctx - m)
    den = jnp.sum(p_loc, axis=-1, keepdims=True) + jnp.sum(p_ctx, axis=-1, keepdims=True)
    o = _dot(p_loc.astype(BF16), vb) + _dot(p_ctx.astype(BF16), vc_ref[...].astype(BF16))
    o_ref[...] = o / den


def _na_attention(proj, bias, rb_type, bstart, n_batch, seq, ctx_len, heads, dh, kr, rb_rows):
    width = GRID_W
    nrb = seq // (rb_rows * width)
    qt = rb_rows * width
    band_tokens = (rb_rows + kr) * width
    ctx_blk0 = n_batch * seq // ctx_len
    grid_spec = pltpu.PrefetchScalarGridSpec(
        num_scalar_prefetch=2,
        grid=(n_batch, heads, nrb),
        in_specs=[
            pl.BlockSpec((qt, dh), lambda b, h, r, t, s: (b * nrb + r, h)),
            pl.BlockSpec((seq, dh), lambda b, h, r, t, s: (b, heads + h)),
            pl.BlockSpec((seq, dh), lambda b, h, r, t, s: (b, 2 * heads + h)),
            pl.BlockSpec((ctx_len, dh), lambda b, h, r, t, s: (ctx_blk0 + b, heads + h)),
            pl.BlockSpec((ctx_len, dh), lambda b, h, r, t, s: (ctx_blk0 + b, 2 * heads + h)),
            pl.BlockSpec((1, 1, qt, band_tokens), lambda b, h, r, t, s: (t[r], h, 0, 0)),
        ],
        out_specs=pl.BlockSpec((qt, dh), lambda b, h, r, t, s: (b * nrb + r, h)),
    )
    return pl.pallas_call(
        functools.partial(_na_kernel, band_tokens=band_tokens, width=width, scale=dh ** -0.5),
        grid_spec=grid_spec,
        out_shape=jax.ShapeDtypeStruct((n_batch * seq, heads * dh), F32),
        compiler_params=_cp(("arbitrary", "arbitrary", "arbitrary")),
        name="na_attention",
    )(rb_type, bstart, proj, proj, proj, proj, proj, bias)


def _ctx_attn_kernel(q_ref, k_ref, v_ref, o_ref, *, scale):
    s = _dot_t(q_ref[...].astype(BF16), k_ref[...].astype(BF16)) * scale
    m = jnp.max(s, axis=-1, keepdims=True)
    p = jnp.exp(s - m)
    den = jnp.sum(p, axis=-1, keepdims=True)
    o_ref[...] = _dot(p.astype(BF16), v_ref[...].astype(BF16)) / den


def _ctx_attention(proj, n_batch, seq, ctx_len, heads, dh):
    ctx_blk0 = n_batch * seq // ctx_len
    return pl.pallas_call(
        functools.partial(_ctx_attn_kernel, scale=dh ** -0.5),
        grid=(n_batch, heads),
        in_specs=[pl.BlockSpec((ctx_len, dh), lambda b, h: (ctx_blk0 + b, h)),
                  pl.BlockSpec((ctx_len, dh), lambda b, h: (ctx_blk0 + b, heads + h)),
                  pl.BlockSpec((ctx_len, dh), lambda b, h: (ctx_blk0 + b, 2 * heads + h))],
        out_specs=pl.BlockSpec((ctx_len, dh), lambda b, h: (b, h)),
        out_shape=jax.ShapeDtypeStruct((n_batch * ctx_len, heads * dh), F32),
        compiler_params=_cp(("arbitrary", "arbitrary")),
        name="ctx_attention",
    )(proj, proj, proj)


def _sgu_kernel(su_ref, sv_ref, w_ref, b_ref, g_ref, o_ref, *, chunk, heads, hd):
    u = _gelu(su_ref[...])
    gv = _gelu(sv_ref[...])
    v = (gv * lax.rsqrt(jnp.mean(gv * gv, axis=-1, keepdims=True) + EPS) * g_ref[...]).astype(BF16)
    for n in range(u.shape[0] // chunk):
        rs = slice(n * chunk, (n + 1) * chunk)
        for h in range(heads):
            cs = slice(h * hd, (h + 1) * hd)
            mixed = _dot(w_ref[h], v[rs, cs]) + b_ref[h]
            o_ref[rs, cs] = u[rs, cs] * mixed


def _sgu(proj, sgu_w, sgu_b, sgu_norm, col_blk, tm):
    n = proj.shape[0]
    heads, chunk, _ = sgu_w.shape
    sw = sgu_norm.shape[-1]
    hd = sw // heads
    bias = jnp.broadcast_to(sgu_b[:, :, None], (heads, chunk, hd))
    return pl.pallas_call(
        functools.partial(_sgu_kernel, chunk=chunk, heads=heads, hd=hd),
        grid=(n // tm,),
        in_specs=[pl.BlockSpec((tm, sw), lambda i: (i, col_blk)),
                  pl.BlockSpec((tm, sw), lambda i: (i, col_blk + 1)),
                  pl.BlockSpec((heads, chunk, chunk), lambda i: (0, 0, 0)),
                  pl.BlockSpec((heads, chunk, hd), lambda i: (0, 0, 0)),
                  pl.BlockSpec((1, sw), lambda i: (0, 0))],
        out_specs=pl.BlockSpec((tm, sw), lambda i: (i, 0)),
        out_shape=jax.ShapeDtypeStruct((n, sw), F32),
        compiler_params=_cp(("arbitrary",)),
        name="sgu",
    )(proj, proj, sgu_w.astype(BF16), bias, sgu_norm.reshape(1, sw))


def _s5_prep_kernel(are_ref, aim_ref, ldt_ref, btr_ref, bti_ref, cr_ref, ci_ref,
                    mt_ref, bp_ref, cp_ref, lam_ref, *, t_len, n_h, n_p):
    rev = pl.program_id(1) == 1
    sgn = jnp.where(rev, -1.0, 1.0)
    are = jnp.minimum(are_ref[0, 0, 0], -1e-4)
    aim = aim_ref[0, 0, 0]
    dt = jnp.exp(ldt_ref[0, 0, 0])
    lr = are * dt
    li = aim * dt
    first = lax.broadcasted_iota(jnp.int32, (1, 2 * n_p), 1) < n_p

    def power(n):
        mag = jnp.exp(n * lr)
        ang = n * li
        return mag * jnp.cos(ang), mag * jnp.sin(ang)

    lbr = jnp.exp(lr) * jnp.cos(li)
    lbi = jnp.exp(lr) * jnp.sin(li)
    den = are * are + aim * aim
    nr = lbr - 1.0
    kr_ = (nr * are + lbi * aim) / den
    ki_ = (lbi * are - nr * aim) / den
    btr = btr_ref[0, 0, 0]
    bti = bti_ref[0, 0, 0]
    bbr = kr_ * btr - ki_ * bti
    bbi = kr_ * bti + ki_ * btr
    ccr = cr_ref[0, 0, 0]
    cci = ci_ref[0, 0, 0]
    b_a = jnp.where(first, bbr, -bbi)
    b_b = jnp.where(first, -bbi, -bbr)
    c_a = jnp.where(first, ccr, cci)
    c_b = jnp.where(first, -cci, ccr)
    b_ri_a = jnp.where(first, bbr, bbi)
    b_ri_b = jnp.where(first, -bbi, bbr)
    c_n_a = jnp.where(first, ccr, -cci)
    c_n_b = jnp.where(first, -cci, -ccr)

    th = t_len * n_h
    tt = lax.broadcasted_iota(jnp.int32, (t_len, 1), 0).astype(F32)
    half = 0.5 * t_len
    n_out = jnp.where(rev, tt, (t_len - 1.0) - tt)
    n_in = jnp.where(rev, t_len - tt, tt + 1.0)
    powers = []
    for n in (sgn * (half - tt), sgn * (tt - half), n_out, n_in):
        powers.extend(power(n))
    rep = (lax.broadcasted_iota(jnp.int32, (th, t_len), 0) // n_h
           == lax.broadcasted_iota(jnp.int32, (th, t_len), 1)).astype(BF16)
    wide = sum(_dot(rep, part) for part in _bf16_pieces(jnp.concatenate(powers, axis=1), 3))
    pw = [wide[:, k * 2 * n_p:(k + 1) * 2 * n_p] for k in range(8)]

    def expand(pr, pi_, xa, xb):
        return (pr * jnp.concatenate([xa] * t_len, axis=0)
                + pi_ * jnp.concatenate([xb] * t_len, axis=0))

    a_cat = expand(pw[0], pw[1], b_a, b_b)
    c_cat = expand(pw[2], pw[3], c_a, c_b)
    a_hi, a_lo = _bf16_pieces(a_cat, 2)
    c_hi, c_lo = _bf16_pieces(c_cat, 2)
    mt = _dot_t(a_hi, c_hi) + (_dot_t(a_hi, c_lo) + _dot_t(a_lo, c_hi))
    rs = lax.broadcasted_iota(jnp.int32, (th, th), 0) // n_h
    ct = lax.broadcasted_iota(jnp.int32, (th, th), 1) // n_h
    keep = (ct - rs) * jnp.where(rev, -1, 1) >= 0
    mt_ref[0, 0, 0] = jnp.where(keep, mt, 0.0).astype(mt_ref.dtype)

    bp_ref[0, 0, 0] = expand(pw[4], pw[5], b_ri_a, b_ri_b)
    cp_ref[0, 0, 0] = expand(pw[6], pw[7], c_n_a, c_n_b)
    ltr, lti = power(jnp.full((1, 1), float(t_len), F32))
    lam_ref[0, 0, 0, 0:1, :] = ltr
    lam_ref[0, 0, 0, 1:2, :] = jnp.where(first, -lti, lti)


def _s5_prep(a_re, a_im, log_dt, b_re, b_im, c_re, c_im, t_len):
    depth, _, g, p, h = b_re.shape
    dup = lambda t: jnp.concatenate([t, t], axis=-1)
    are = dup(a_re).reshape(depth, 2, g, 1, 2 * p)
    aim = dup(a_im).reshape(depth, 2, g, 1, 2 * p)
    ldt = log_dt.reshape(depth, 2, g, 1, 1)
    btr = dup(jnp.swapaxes(b_re, -1, -2))
    bti = dup(jnp.swapaxes(b_im, -1, -2))
    ccr = dup(c_re)
    cci = dup(c_im)
    th = t_len * h
    row = lambda r, c: pl.BlockSpec((1, 1, 1, r, c), lambda l, d, k: (l, d, k, 0, 0))
    return pl.pallas_call(
        functools.partial(_s5_prep_kernel, t_len=t_len, n_h=h, n_p=p),
        grid=(depth, 2, g),
        in_specs=[row(1, 2 * p), row(1, 2 * p), row(1, 1), row(h, 2 * p), row(h, 2 * p),
                  row(h, 2 * p), row(h, 2 * p)],
        out_specs=[row(th, th), row(th, 2 * p), row(th, 2 * p), row(2, 2 * p)],
        out_shape=[jax.ShapeDtypeStruct((depth, 2, g, th, th), BF16),
                   jax.ShapeDtypeStruct((depth, 2, g, th, 2 * p), F32),
                   jax.ShapeDtypeStruct((depth, 2, g, th, 2 * p), F32),
                   jax.ShapeDtypeStruct((depth, 2, g, 2, 2 * p), F32)],
        compiler_params=_cp(("arbitrary", "arbitrary", "arbitrary")),
        name="s5_prep",
    )(are, aim, ldt, btr, bti, ccr, cci)


def _s5_kernel(u_ref, mt_ref, bp_ref, cp_ref, lam_ref, y_ref, hin_ref, *, n_ctx_chunks, n_p, n_batch):
    ub = u_ref[0]
    uf = ub.astype(F32)
    per = 8 // n_batch
    n_tiles = hin_ref.shape[1]
    n_chunks = n_tiles * per
    row = lax.broadcasted_iota(jnp.int32, (8, 2 * n_p), 0)
    hloc = [_dot(uf, bp_ref[d, 0], precision=HI).reshape(n_tiles, 8, 2 * n_p) for d in range(2)]
    orders = [list(range(n_chunks)),
              list(range(n_ctx_chunks - 1, -1, -1)) + list(range(n_chunks - 1, n_ctx_chunks - 1, -1))]
    h = [jnp.zeros((8, 2 * n_p), F32) for _ in range(2)]
    tile = [None, None]
    for i in range(n_chunks):
        for d in range(2):
            g, q = divmod(orders[d][i], per)
            placed = h[d] if q == 0 else pltpu.roll(h[d], q * n_batch, axis=0)
            keep = (row >= q * n_batch) & (row < (q + 1) * n_batch)
            tile[d] = jnp.where(keep, placed, 0.0 if i % per == 0 else tile[d])
            if i % per == per - 1:
                hin_ref[d, g] = tile[d]
            loc = hloc[d][g] if q == 0 else pltpu.roll(hloc[d][g], 8 - q * n_batch, axis=0)
            h[d] = (h[d] * lam_ref[d, 0, 0:1, :] + pltpu.roll(h[d], n_p, axis=1) * lam_ref[d, 0, 1:2, :]
                    + loc)
    y = _dot(ub, mt_ref[0, 0]) + _dot(ub, mt_ref[1, 0])
    for d in range(2):
        hin = hin_ref[d].reshape(n_tiles * 8, 2 * n_p)
        y = y + _dot_t(hin, cp_ref[d, 0], precision=HI)
    y_ref[0] = y


def _s5_scan(u_g, mt, bp, cp, lam, n_ctx_chunks, n_batch):
    g, r, th = u_g.shape
    p2 = bp.shape[-1]
    per = 8 // n_batch
    assert 8 % n_batch == 0 and n_ctx_chunks % per == 0 and r % 8 == 0
    return pl.pallas_call(
        functools.partial(_s5_kernel, n_ctx_chunks=n_ctx_chunks, n_p=p2 // 2, n_batch=n_batch),
        grid=(g,),
        in_specs=[pl.BlockSpec((1, r, th), lambda k: (k, 0, 0)),
                  pl.BlockSpec((2, 1, th, th), lambda k: (0, k, 0, 0)),
                  pl.BlockSpec((2, 1, th, p2), lambda k: (0, k, 0, 0)),
                  pl.BlockSpec((2, 1, th, p2), lambda k: (0, k, 0, 0)),
                  pl.BlockSpec((2, 1, 2, p2), lambda k: (0, k, 0, 0))],
        out_specs=pl.BlockSpec((1, r, th), lambda k: (k, 0, 0)),
        out_shape=jax.ShapeDtypeStruct((g, r, th), F32),
        scratch_shapes=[pltpu.VMEM((2, r // 8, 8, p2), F32)],
        compiler_params=_cp(("arbitrary",)),
        name="s5_scan",
    )(u_g, mt, bp, cp, lam)


def _s5_to_groups(s_in, n_batch, seq, ctx_len, g, h, t_len):
    n_lat = n_batch * seq
    lat = s_in[:n_lat].reshape(n_batch, seq // t_len, t_len, g, h)
    cx = s_in[n_lat:].reshape(n_batch, ctx_len // t_len, t_len, g, h)
    al = jnp.concatenate([cx, lat], axis=1)
    al = jnp.transpose(al, (3, 1, 0, 2, 4))
    return al.reshape(g, al.shape[1] * n_batch, t_len * h)


def _s5_from_groups(y_g, n_batch, seq, ctx_len, g, h, t_len):
    n_chunks = y_g.shape[1] // n_batch
    y = y_g.reshape(g, n_chunks, n_batch, t_len, h)
    y = jnp.transpose(y, (2, 1, 3, 0, 4))
    ncc = ctx_len // t_len
    cx = y[:, :ncc].reshape(n_batch * ctx_len, g * h)
    lat = y[:, ncc:].reshape(n_batch * seq, g * h)
    return jnp.concatenate([lat, cx], axis=0)


def _merge_kernel(a_ref, s_ref, y_ref, u_ref, d_ref, gw_ref, gb_ref, n_ref, o_ref, *, wa, ws):
    def rms(t, g):
        return t * lax.rsqrt(jnp.mean(t * t, axis=-1, keepdims=True) + EPS) * g

    y = _gelu(y_ref[...] + u_ref[...] * d_ref[...])
    z = y * jax.nn.sigmoid(_dot(y.astype(BF16), gw_ref[...]) + gb_ref[...])
    o_ref[:, :wa] = rms(a_ref[...], n_ref[:, :wa]).astype(o_ref.dtype)
    o_ref[:, wa:wa + ws] = rms(s_ref[...], n_ref[:, wa:wa + ws]).astype(o_ref.dtype)
    o_ref[:, wa + ws:] = rms(z, n_ref[:, wa + ws:]).astype(o_ref.dtype)


def _merge(attn, sgu, y5, proj, s5_d, glu_w, glu_b, out_norm, tm):
    n, wa = attn.shape
    ws = sgu.shape[1]
    w5 = y5.shape[1]
    wt = wa + ws + w5
    u_blk = proj.shape[1] // w5 - 1
    return pl.pallas_call(
        functools.partial(_merge_kernel, wa=wa, ws=ws),
        grid=(n // tm,),
        in_specs=[pl.BlockSpec((tm, wa), lambda i: (i, 0)),
                  pl.BlockSpec((tm, ws), lambda i: (i, 0)),
                  pl.BlockSpec((tm, w5), lambda i: (i, 0)),
                  pl.BlockSpec((tm, w5), lambda i: (i, u_blk)),
                  pl.BlockSpec((1, w5), lambda i: (0, 0)),
                  pl.BlockSpec((w5, w5), lambda i: (0, 0)),
                  pl.BlockSpec((1, w5), lambda i: (0, 0)),
                  pl.BlockSpec((1, wt), lambda i: (0, 0))],
        out_specs=pl.BlockSpec((tm, wt), lambda i: (i, 0)),
        out_shape=jax.ShapeDtypeStruct((n, wt), BF16),
        compiler_params=_cp(("arbitrary",)),
        name="merge_norm",
    )(attn, sgu, y5, proj, s5_d.reshape(1, w5), glu_w.astype(BF16), glu_b.reshape(1, w5),
      out_norm.reshape(1, wt))


def _topk_cols(ss, k, payloads=None):
    n, t = ss[0].shape
    iota = lax.broadcasted_iota(jnp.int32, (n, t), 0).astype(F32)
    krow = lax.broadcasted_iota(jnp.int32, (k, t), 0)
    ss = list(ss)
    vals = [jnp.zeros((k, t), F32) for _ in ss]
    outs = [jnp.zeros((k, t), F32) for _ in ss]
    for it in range(k):
        for p in range(len(ss)):
            s = ss[p]
            m = jnp.max(s, axis=0, keepdims=True)
            pos = jnp.min(jnp.where(s == m, iota, float(n)), axis=0, keepdims=True)
            hit = iota == pos
            if payloads is None:
                sel = pos
            else:
                sel = jnp.max(jnp.where(hit, payloads[p], -1.0), axis=0, keepdims=True)
            vals[p] = jnp.where(krow == it, m, vals[p])
            outs[p] = jnp.where(krow == it, sel, outs[p])
            ss[p] = jnp.where(hit, -jnp.inf, s)
    return list(zip(vals, outs))


def _staircase_candidates(v1, i1, v2, i2, k, nkeys):
    s = math.isqrt(k)
    t = v1.shape[1]
    r = lax.broadcasted_iota(jnp.int32, (8, t), 0)
    sc, ex = [], []

    def emit(vs, es, lo, hi):
        ok = (r >= lo) & (r < hi)
        sc.append(jnp.where(ok, vs, -jnp.inf))
        ex.append(es)

    for b in range(s):
        na = k // (b + 1)
        for a0 in range(0, na, 8):
            emit(v1[a0:a0 + 8] + v2[b:b + 1], i1[a0:a0 + 8] * nkeys + i2[b:b + 1], 0, na - a0)
    for a in range(s):
        nb = k // (a + 1)
        for b0 in range((s // 8) * 8, nb, 8):
            if nb - b0 > max(s - b0, 0):
                emit(v1[a:a + 1] + v2[b0:b0 + 8], i1[a:a + 1] * nkeys + i2[b0:b0 + 8], s - b0, nb - b0)
    return jnp.concatenate(sc, axis=0), jnp.concatenate(ex, axis=0)


def _peer_sel_kernel(h_ref, wq_ref, k1_ref, k2_ref, ids_ref, gate_ref, q_scr, *, heads, qdim, nkeys, topk, tl):
    q_scr[...] = _dot(h_ref[...], wq_ref[...])
    half = qdim // 2

    def head_body(hh, carry):
        n_th = h_ref.shape[0] // tl
        scores = []
        for th in range(n_th):
            q1 = q_scr[pl.ds(th * tl, tl), pl.ds(pl.multiple_of(hh * qdim, qdim), half)]
            q2 = q_scr[pl.ds(th * tl, tl), pl.ds(pl.multiple_of(hh * qdim + half, half), half)]
            scores.append(_dot_t(k1_ref[hh], q1, precision=HI))
            scores.append(_dot_t(k2_ref[hh], q2, precision=HI))
        tops = _topk_cols(scores, topk)
        cands = [_staircase_candidates(*tops[2 * th], *tops[2 * th + 1], topk, float(nkeys))
                 for th in range(n_th)]
        finals = _topk_cols([c[0] for c in cands], topk, payloads=[c[1] for c in cands])
        for th in range(n_th):
            best, expert = finals[th]
            e = jnp.exp(best - best[0:1])
            gate = e / jnp.sum(e, axis=0, keepdims=True)
            rows = pl.ds(pl.multiple_of(hh * topk, topk), topk)
            ids_ref[rows, pl.ds(th * tl, tl)] = expert.astype(jnp.int32)
            gate_ref[rows, pl.ds(th * tl, tl)] = gate
        return carry

    lax.fori_loop(0, heads, head_body, 0)


def _peer_select(h, wq, k1, k2, tm, tl=128):
    n, d = h.shape
    heads, nkeys, half = k1.shape
    qdim = 2 * half
    nsel = heads * PEER_TOPK
    return pl.pallas_call(
        functools.partial(_peer_sel_kernel, heads=heads, qdim=qdim, nkeys=nkeys, topk=PEER_TOPK, tl=tl),
        grid=(n // tm,),
        in_specs=[pl.BlockSpec((tm, d), lambda i: (i, 0)),
                  pl.BlockSpec((d, heads * qdim), lambda i: (0, 0)),
                  pl.BlockSpec((heads, nkeys, half), lambda i: (0, 0, 0)),
                  pl.BlockSpec((heads, nkeys, half), lambda i: (0, 0, 0))],
        out_specs=[pl.BlockSpec((nsel, tm), lambda i: (0, i)),
                   pl.BlockSpec((nsel, tm), lambda i: (0, i))],
        out_shape=[jax.ShapeDtypeStruct((nsel, n), jnp.int32),
                   jax.ShapeDtypeStruct((nsel, n), F32)],
        scratch_shapes=[pltpu.VMEM((tm, heads * qdim), F32)],
        compiler_params=_cp(("arbitrary",)),
        name="peer_select",
    )(h, wq, k1, k2)


def _pack_tables(u, v):
    ub = lax.bitcast_convert_type(u.astype(BF16), jnp.uint16).astype(jnp.uint32)
    vb = lax.bitcast_convert_type(v.astype(BF16), jnp.uint16).astype(jnp.uint32)
    return (ub << 16) | vb


def _peer_ffn_kernel(ids_ref, idn_ref, g_ref, x_ref, n_ref, m_ref, tab_ref, o_ref, buf0_ref, buf1_ref, sem_ref,
                     *, tg, nsel, row):
    i = pl.program_id(0)
    nsteps = pl.num_programs(0)
    m = m_ref[0]
    eye = (lax.broadcasted_iota(jnp.int32, (nsel, nsel), 0)
           == lax.broadcasted_iota(jnp.int32, (nsel, nsel), 1))

    def row_copy(e, buf, slot, k, j):
        return pltpu.make_async_copy(tab_ref.at[pl.ds(e, 1)], buf.at[pl.ds(k * nsel + j, 1)],
                                     sem_ref.at[slot, k])

    def token_wait(buf, slot, k):
        pltpu.make_async_copy(tab_ref.at[pl.ds(0, nsel)], buf.at[pl.ds(k * nsel, nsel)],
                              sem_ref.at[slot, k]).wait()

    @pl.when(i == 0)
    def _():
        for k in range(tg):
            def body(j, carry, k=k):
                row_copy(ids_ref[k, j], buf0_ref, 0, k, j).start()
                return carry
            lax.fori_loop(0, nsel, body, 0)

    def step(cur, nxt, cs, ns):
        for k in range(tg):
            token_wait(cur, cs, k)
            for j in range(nsel):
                row_copy(idn_ref[k, j], nxt, ns, k, j).start()
            xk = x_ref[k:k + 1, :]
            hk = xk * lax.rsqrt(jnp.mean(xk * xk, axis=-1, keepdims=True) + EPS) * n_ref[...]
            hk = hk * (1.0 + m[row + 1:row + 2]) + m[row:row + 1]
            w = cur[k * nsel:(k + 1) * nsel, :]
            uf = lax.bitcast_convert_type(w & jnp.uint32(0xFFFF0000), F32)
            act = _gelu(jnp.sum(uf * hk, axis=1, keepdims=True))
            gcol = jnp.sum(jnp.where(eye, g_ref[k:k + 1, :], 0.0), axis=1, keepdims=True)
            vf = lax.bitcast_convert_type(w << 16, F32)
            out = jnp.sum(vf * (gcol * act), axis=0, keepdims=True)
            o_ref[k:k + 1, :] = xk + m[row + 2:row + 3] * out

        @pl.when(i == nsteps - 1)
        def _():
            for k in range(tg):
                token_wait(nxt, ns, k)

    @pl.when(i % 2 == 0)
    def _():
        step(buf0_ref, buf1_ref, 0, 1)

    @pl.when(i % 2 == 1)
    def _():
        step(buf1_ref, buf0_ref, 1, 0)


def _peer_ffn(x, ids, gates, norm_g, mod, row, tab, lat_blocks, tg=8):
    n, d = x.shape
    nsel = ids.shape[1]
    nlb, bpb, nb = lat_blocks
    steps = n // tg
    return pl.pallas_call(
        functools.partial(_peer_ffn_kernel, tg=tg, nsel=nsel, row=row),
        grid=(steps,),
        in_specs=[pl.BlockSpec((tg, nsel), lambda i: (i, 0), memory_space=pltpu.SMEM),
                  pl.BlockSpec((tg, nsel), lambda i: (jnp.minimum(i + 1, steps - 1), 0),
                               memory_space=pltpu.SMEM),
                  pl.BlockSpec((tg, nsel), lambda i: (i, 0)),
                  pl.BlockSpec((tg, d), lambda i: (i, 0)),
                  pl.BlockSpec((1, d), lambda i: (0, 0)),
                  pl.BlockSpec((1, mod.shape[1], d), lambda i: (_mod_row(i, nlb, bpb, nb), 0, 0)),
                  pl.BlockSpec(memory_space=pl.ANY)],
        out_specs=pl.BlockSpec((tg, d), lambda i: (i, 0)),
        out_shape=jax.ShapeDtypeStruct((n, d), F32),
        scratch_shapes=[pltpu.VMEM((tg * nsel, d), jnp.uint32),
                        pltpu.VMEM((tg * nsel, d), jnp.uint32),
                        pltpu.SemaphoreType.DMA((2, tg))],
        compiler_params=_cp(("arbitrary",)),
        name="peer_ffn",
    )(ids, ids, gates, x, norm_g.reshape(1, d), mod, tab)


def _final_norm_kernel(x_ref, g_ref, o_ref):
    x = x_ref[...]
    o_ref[...] = x * lax.rsqrt(jnp.mean(x * x, axis=-1, keepdims=True) + EPS) * g_ref[...]


def _final_norm(x, g, n_rows, tm):
    d = x.shape[1]
    return pl.pallas_call(
        _final_norm_kernel,
        grid=(n_rows // tm,),
        in_specs=[pl.BlockSpec((tm, d), lambda i: (i, 0)),
                  pl.BlockSpec((1, d), lambda i: (0, 0))],
        out_specs=pl.BlockSpec((tm, d), lambda i: (i, 0)),
        out_shape=jax.ShapeDtypeStruct((n_rows, d), F32),
        compiler_params=_cp(("arbitrary",)),
        name="final_norm",
    )(x, g.reshape(1, d))


def kernel(x, c, ctx, c_ctx, ada_w, ada_b, norm_mix, norm_ffn, w_in, w_out, out_norm, na_rpb, sgu_w, sgu_b,
           sgu_norm, s5_a_re, s5_a_im, s5_log_dt, s5_b_re, s5_b_im, s5_c_re, s5_c_im, s5_d, s5_glu_w,
           s5_glu_b, peer_wq, peer_k1, peer_k2, peer_u, peer_v, final_norm):
    n_batch, seq, d = x.shape
    ctx_len = ctx.shape[1]
    depth = ada_w.shape[0]
    n_mod = ada_w.shape[2] // d
    in_w = w_in.shape[2]
    sgu_width = sgu_norm.shape[-1]
    s5_width = s5_d.shape[-1]
    na_width = (in_w - 2 * sgu_width - s5_width) // 3
    na_heads = na_rpb.shape[1]
    dh = na_width // na_heads
    kr_full = (na_rpb.shape[2] + 1) // 2
    wc = (na_rpb.shape[3] + 1) // 2
    rows = seq // GRID_W
    kr = min(kr_full, rows)
    _, _, s5_g, s5_p, s5_h = s5_b_re.shape
    n_lat = n_batch * seq
    n_tok = n_lat + n_batch * ctx_len

    tm = 256
    lat_blocks = (n_lat // tm, seq // tm, n_batch)
    tg = 8
    grp_blocks = (n_lat // tg, seq // tg, n_batch)

    cond = jnp.concatenate([c, c_ctx[None], jnp.zeros((8 - n_batch - 1, d), F32)], axis=0)
    mods = _ada_mod(cond, ada_w, ada_b).reshape(depth, 8, n_mod, d)

    rb_rows = 8
    types, rb_type, bstart, co, valid = _na_tables(rows, GRID_W, kr, wc, rb_rows)
    types_ref = np.where(types == 2 * kr - 1, 2 * kr_full - 1, types + (kr_full - kr))
    s5_ops = _s5_prep(s5_a_re, s5_a_im, s5_log_dt, s5_b_re, s5_b_im, s5_c_re, s5_c_im, S5_T)

    xs = jnp.concatenate([x.reshape(n_lat, d), ctx.reshape(n_batch * ctx_len, d)], axis=0)
    for l in range(depth):
        mod = mods[l]
        h = _norm_mod(xs, norm_mix[l], mod, 0, lat_blocks, tm)
        proj = _matmul(h, w_in[l].astype(BF16), _tile(n_tok, 512, 8), _tile(in_w, 1152, 128))

        bias = _na_bias(na_rpb[l], types_ref, co, valid)
        attn_lat = _na_attention(proj, bias, jnp.asarray(rb_type), jnp.asarray(bstart), n_batch, seq, ctx_len,
                                 na_heads, dh, kr, rb_rows)
        attn_ctx = _ctx_attention(proj, n_batch, seq, ctx_len, na_heads, dh)
        attn = jnp.concatenate([attn_lat, attn_ctx], axis=0)

        sgu = _sgu(proj, sgu_w[l], sgu_b[l], sgu_norm[l], 3 * na_width // sgu_width, tm)

        s_in = proj[:, in_w - s5_width:].astype(BF16)
        u_g = _s5_to_groups(s_in, n_batch, seq, ctx_len, s5_g, s5_h, S5_T)
        y_g = _s5_scan(u_g, s5_ops[0][l], s5_ops[1][l], s5_ops[2][l], s5_ops[3][l], ctx_len // S5_T, n_batch)
        y5 = _s5_from_groups(y_g, n_batch, seq, ctx_len, s5_g, s5_h, S5_T)

        merged = _merge(attn, sgu, y5, proj, s5_d[l], s5_glu_w[l], s5_glu_b[l], out_norm[l], tm)
        xs = _matmul_res(merged, w_out[l].astype(BF16), xs, mod, 2, lat_blocks, tm, _tile(d, 1024, 128))

        if l == depth - 1:
            xs = xs[:n_lat]
        hf = _norm_mod(xs, norm_ffn[l], mod, 3, lat_blocks, tm)
        ids_t, gates_t = _peer_select(hf, peer_wq[l].astype(BF16), peer_k1[l], peer_k2[l], tm)
        tab = _pack_tables(peer_u[l], peer_v[l])
        xs = _peer_ffn(xs, ids_t.T, gates_t.T, norm_ffn[l], mod, 3, tab, grp_blocks, tg)

    out = _final_norm(xs, final_norm, n_lat, tm)
    return out.reshape(n_batch, seq, d)
```

```python
import functools
import math

import numpy as np
import jax
import jax.numpy as jnp
from jax import lax
from jax.experimental import pallas as pl
from jax.experimental.pallas import tpu as pltpu

F32 = jnp.float32
BF16 = jnp.bfloat16
EPS = 1e-6
GRID_W = 64
PEER_TOPK = 16
S5_T = 64
NEG = -1e30
HI = lax.Precision.HIGHEST
VMEM_LIMIT = 56 * 1024 * 1024


def _cp(sem, vmem=VMEM_LIMIT):
    return pltpu.CompilerParams(dimension_semantics=sem, vmem_limit_bytes=vmem)


def _tile(n, cap, unit):
    return max(t for t in range(unit, min(n, cap) + 1, unit) if n % t == 0)


def _gelu(x):
    return 0.5 * x * (1.0 + lax.erf(x * (1.0 / math.sqrt(2.0))))


def _dot_t(a, b, precision=None):
    return lax.dot_general(a, b, (((1,), (1,)), ((), ())), precision=precision,
                           preferred_element_type=F32)


def _dot(a, b, precision=None):
    return lax.dot_general(a, b, (((1,), (0,)), ((), ())), precision=precision,
                           preferred_element_type=F32)


def _bf16_pieces(x, n):
    out = []
    for _ in range(n):
        p = x.astype(BF16)
        out.append(p)
        x = x - p.astype(F32)
    return out


def _mod_row(i, n_lat_blocks, blocks_per_batch, n_batch):
    return jnp.where(i < n_lat_blocks, i // blocks_per_batch, n_batch)


def _ada_kernel(c_ref, w_ref, b_ref, o_ref):
    c = c_ref[...]
    s = (c * jax.nn.sigmoid(c)).astype(BF16)
    o_ref[0] = _dot(s, w_ref[0].astype(BF16)) + b_ref[0]


def _ada_mod(cond, ada_w, ada_b, tn=1024):
    depth, d, nm = ada_w.shape
    rows = cond.shape[0]
    return pl.pallas_call(
        _ada_kernel,
        grid=(depth, nm // tn),
        in_specs=[pl.BlockSpec((rows, d), lambda l, j: (0, 0)),
                  pl.BlockSpec((1, d, tn), lambda l, j: (l, 0, j)),
                  pl.BlockSpec((1, 1, tn), lambda l, j: (l, 0, j))],
        out_specs=pl.BlockSpec((1, rows, tn), lambda l, j: (l, 0, j)),
        out_shape=jax.ShapeDtypeStruct((depth, rows, nm), F32),
        compiler_params=_cp(("arbitrary", "arbitrary")),
        name="ada_mod",
    )(cond, ada_w, ada_b.reshape(depth, 1, nm))


def _norm_mod_kernel(x_ref, g_ref, m_ref, o_ref, *, row):
    x = x_ref[...]
    m = m_ref[0]
    y = x * lax.rsqrt(jnp.mean(x * x, axis=-1, keepdims=True) + EPS) * g_ref[...]
    o_ref[...] = (y * (1.0 + m[row + 1:row + 2]) + m[row:row + 1]).astype(o_ref.dtype)


def _norm_mod(x, g, mod, row, lat_blocks, tm):
    n, d = x.shape
    nlb, bpb, nb = lat_blocks
    return pl.pallas_call(
        functools.partial(_norm_mod_kernel, row=row),
        grid=(n // tm,),
        in_specs=[pl.BlockSpec((tm, d), lambda i: (i, 0)),
                  pl.BlockSpec((1, d), lambda i: (0, 0)),
                  pl.BlockSpec((1, mod.shape[1], d), lambda i: (_mod_row(i, nlb, bpb, nb), 0, 0))],
        out_specs=pl.BlockSpec((tm, d), lambda i: (i, 0)),
        out_shape=jax.ShapeDtypeStruct((n, d), BF16),
        compiler_params=_cp(("arbitrary",)),
        name="norm_mod",
    )(x, g.reshape(1, d), mod)


def _mm_kernel(a_ref, w_ref, o_ref):
    o_ref[...] = _dot(a_ref[...], w_ref[...])


def _mm_res_kernel(a_ref, w_ref, r_ref, m_ref, o_ref, *, row):
    o_ref[...] = r_ref[...] + m_ref[0][row:row + 1] * _dot(a_ref[...], w_ref[...])


def _matmul(a, w, tm, tn):
    m, k = a.shape
    nn = w.shape[1]
    return pl.pallas_call(
        _mm_kernel,
        grid=(nn // tn, m // tm),
        in_specs=[pl.BlockSpec((tm, k), lambda j, i: (i, 0)),
                  pl.BlockSpec((k, tn), lambda j, i: (0, j))],
        out_specs=pl.BlockSpec((tm, tn), lambda j, i: (i, j)),
        out_shape=jax.ShapeDtypeStruct((m, nn), F32),
        compiler_params=_cp(("arbitrary", "arbitrary")),
        name="matmul",
    )(a, w)


def _matmul_res(a, w, res, mod, row, lat_blocks, tm, tn):
    m, k = a.shape
    nn = w.shape[1]
    nlb, bpb, nb = lat_blocks
    return pl.pallas_call(
        functools.partial(_mm_res_kernel, row=row),
        grid=(nn // tn, m // tm),
        in_specs=[pl.BlockSpec((tm, k), lambda j, i: (i, 0)),
                  pl.BlockSpec((k, tn), lambda j, i: (0, j)),
                  pl.BlockSpec((tm, tn), lambda j, i: (i, j)),
                  pl.BlockSpec((1, mod.shape[1], tn), lambda j, i: (_mod_row(i, nlb, bpb, nb), 0, j))],
        out_specs=pl.BlockSpec((tm, tn), lambda j, i: (i, j)),
        out_shape=jax.ShapeDtypeStruct((m, nn), F32),
        compiler_params=_cp(("arbitrary", "arbitrary")),
        name="matmul_res",
    )(a, w, res, mod)


def _na_tables(rows, width, kr, wc, rb_rows):
    band = rb_rows + kr
    nrb = rows // rb_rows
    masked = 2 * kr - 1
    ro = np.full((nrb, rb_rows, band), masked, dtype=np.int32)
    bstart = np.zeros((nrb,), dtype=np.int32)
    for rb in range(nrb):
        bs = int(np.clip(rb_rows * rb - kr // 2, 0, rows - band))
        bstart[rb] = bs
        for dlt in range(rb_rows):
            r = rb_rows * rb + dlt
            rs = int(np.clip(r - kr // 2, 0, rows - kr))
            for j in range(kr):
                ro[rb, dlt, rs + j - bs] = rs + j - r + (kr - 1)
    types, inv = np.unique(ro.reshape(nrb, -1), axis=0, return_inverse=True)
    types = types.reshape(-1, rb_rows, band)
    col = np.arange(width)
    cs = np.clip(col - wc // 2, 0, width - wc)
    kc = np.arange(width)
    valid = (kc[None, :] >= cs[:, None]) & (kc[None, :] < cs[:, None] + wc)
    co = np.clip(kc[None, :] - col[:, None] + (wc - 1), 0, 2 * wc - 2)
    return types, inv.reshape(-1).astype(np.int32), bstart, co, valid


def _na_bias(rpb, types, co, valid):
    h = rpb.shape[0]
    w = co.shape[0]
    tb = jnp.where(valid[None, None], rpb[:, :, co], NEG)
    tb = jnp.concatenate([tb, jnp.full((h, 1, w, w), NEG, F32)], axis=1)
    dense = tb[:, types]
    nt, rbr, band = types.shape
    dense = jnp.transpose(dense, (1, 0, 2, 4, 3, 5))
    return dense.reshape(nt, h, rbr * w, band * w)


def _na_kernel(type_ref, bstart_ref, q_ref, k_ref, v_ref, kc_ref, vc_ref, bias_ref, o_ref, *,
               band_tokens, width, scale):
    rb = pl.program_id(2)
    start = pl.multiple_of(bstart_ref[rb] * width, width)
    q = q_ref[...].astype(BF16)
    kb = k_ref[pl.ds(start, band_tokens), :].astype(BF16)
    vb = v_ref[pl.ds(start, band_tokens), :].astype(BF16)
    s_loc = _dot_t(q, kb) * scale + bias_ref[0, 0]
    s_ctx = _dot_t(q, kc_ref[...].astype(BF16)) * scale
    m = jnp.maximum(jnp.max(s_loc, axis=-1, keepdims=True), jnp.max(s_ctx, axis=-1, keepdims=True))
    p_loc = jnp.exp(s_loc - m)
    p_ctx = jnp.exp(s_ctx - m)
    den = jnp.sum(p_loc, axis=-1, keepdims=True) + jnp.sum(p_ctx, axis=-1, keepdims=True)
    o = _dot(p_loc.astype(BF16), vb) + _dot(p_ctx.astype(BF16), vc_ref[...].astype(BF16))
    o_ref[...] = o / den


def _na_attention(proj, bias, rb_type, bstart, n_batch, seq, ctx_len, heads, dh, kr, rb_rows):
    width = GRID_W
    nrb = seq // (rb_rows * width)
    qt = rb_rows * width
    band_tokens = (rb_rows + kr) * width
    ctx_blk0 = n_batch * seq // ctx_len
    grid_spec = pltpu.PrefetchScalarGridSpec(
        num_scalar_prefetch=2,
        grid=(n_batch, heads, nrb),
        in_specs=[
            pl.BlockSpec((qt, dh), lambda b, h, r, t, s: (b * nrb + r, h)),
            pl.BlockSpec((seq, dh), lambda b, h, r, t, s: (b, heads + h)),
            pl.BlockSpec((seq, dh), lambda b, h, r, t, s: (b, 2 * heads + h)),
            pl.BlockSpec((ctx_len, dh), lambda b, h, r, t, s: (ctx_blk0 + b, heads + h)),
            pl.BlockSpec((ctx_len, dh), lambda b, h, r, t, s: (ctx_blk0 + b, 2 * heads + h)),
            pl.BlockSpec((1, 1, qt, band_tokens), lambda b, h, r, t, s: (t[r], h, 0, 0)),
        ],
        out_specs=pl.BlockSpec((qt, dh), lambda b, h, r, t, s: (b * nrb + r, h)),
    )
    return pl.pallas_call(
        functools.partial(_na_kernel, band_tokens=band_tokens, width=width, scale=dh ** -0.5),
        grid_spec=grid_spec,
        out_shape=jax.ShapeDtypeStruct((n_batch * seq, heads * dh), F32),
        compiler_params=_cp(("arbitrary", "arbitrary", "arbitrary")),
        name="na_attention",
    )(rb_type, bstart, proj, proj, proj, proj, proj, bias)


def _ctx_attn_kernel(q_ref, k_ref, v_ref, o_ref, *, scale):
    s =_dot_t(q_ref[...].astype(BF16), k_ref[...].astype(BF16)) * scale
    m = jnp.max(s, axis=-1, keepdims=True)
    p = jnp.exp(s - m)
    den = jnp.sum(p, axis=-1, keepdims=True)
    o_ref[...] = _dot(p.astype(BF16), v_ref[...].astype(BF16)) / den


def _ctx_attention(proj, n_batch, seq, ctx_len, heads, dh):
    ctx_blk0 = n_batch * seq // ctx_len
    return pl.pallas_call(
        functools.partial(_ctx_attn_kernel, scale=dh ** -0.5),
        grid=(n_batch, heads),
        in_specs=[pl.BlockSpec((ctx_len, dh), lambda b, h: (ctx_blk0 + b, h)),
                  pl.BlockSpec((ctx_len, dh), lambda b, h: (ctx_blk0 + b, heads + h)),
                  pl.BlockSpec((ctx_len, dh), lambda b, h: (ctx_blk0 + b, 2 * heads + h))],
        out_specs=pl.BlockSpec((ctx_len, dh), lambda b, h: (b, h)),
        out_shape=jax.ShapeDtypeStruct((n_batch * ctx_len, heads * dh), F32),
        compiler_params=_cp(("arbitrary", "arbitrary")),
        name="ctx_attention",
    )(proj, proj, proj)


def _sgu_kernel(su_ref, sv_ref, w_ref, b_ref, g_ref, o_ref, *, chunk, heads, hd):
    u = _gelu(su_ref[...])
    gv = _gelu(sv_ref[...])
    v = (gv * lax.rsqrt(jnp.mean(gv * gv, axis=-1, keepdims=True) + EPS) * g_ref[...]).astype(BF16)
    for n in range(u.shape[0] // chunk):
        rs = slice(n * chunk, (n + 1) * chunk)
        for h in range(heads):
            cs = slice(h * hd, (h + 1) * hd)
            mixed = _dot(w_ref[h], v[rs, cs]) + b_ref[h]
            o_ref[rs, cs] = u[rs, cs] * mixed


def _sgu(proj, sgu_w, sgu_b, sgu_norm, col_blk, tm):
    n = proj.shape[0]
    heads, chunk, _ = sgu_w.shape
    sw = sgu_norm.shape[-1]
    hd = sw // heads
    bias = jnp.broadcast_to(sgu_b[:, :, None], (heads, chunk, hd))
    return pl.pallas_call(
        functools.partial(_sgu_kernel, chunk=chunk, heads=heads, hd=hd),
        grid=(n // tm,),
        in_specs=[pl.BlockSpec((tm, sw), lambda i: (i, col_blk)),
                  pl.BlockSpec((tm, sw), lambda i: (i, col_blk + 1)),
                  pl.BlockSpec((heads, chunk, chunk), lambda i: (0, 0, 0)),
                  pl.BlockSpec((heads, chunk, hd), lambda i: (0, 0, 0)),
                  pl.BlockSpec((1, sw), lambda i: (0, 0))],
        out_specs=pl.BlockSpec((tm, sw), lambda i: (i, 0)),
        out_shape=jax.ShapeDtypeStruct((n, sw), F32),
        compiler_params=_cp(("arbitrary",)),
        name="sgu",
    )(proj, proj, sgu_w.astype(BF16), bias, sgu_norm.reshape(1, sw))


def _s5_prep_kernel(are_ref, aim_ref, ldt_ref, btr_ref, bti_ref, cr_ref, ci_ref,
                    mt_ref, bp_ref, cp_ref, lam_ref, *, t_len, n_h, n_p):
    rev = pl.program_id(1) == 1
    sgn = jnp.where(rev, -1.0, 1.0)
    are = jnp.minimum(are_ref[0, 0, 0], -1e-4)
    aim = aim_ref[0, 0, 0]
    dt = jnp.exp(ldt_ref[0, 0, 0])
    lr = are * dt
    li = aim * dt
    first = lax.broadcasted_iota(jnp.int32, (1, 2 * n_p), 1) < n_p

    def power(n):
        mag = jnp.exp(n * lr)
        ang = n * li
        return mag * jnp.cos(ang), mag * jnp.sin(ang)

    lbr = jnp.exp(lr) * jnp.cos(li)
    lbi = jnp.exp(lr) * jnp.sin(li)
    den = are * are + aim * aim
    nr = lbr - 1.0
    kr_ = (nr * are + lbi * aim) / den
    ki_ = (lbi * are - nr * aim) / den
    btr = btr_ref[0, 0, 0]
    bti = bti_ref[0, 0, 0]
    bbr = kr_ * btr - ki_ * bti
    bbi = kr_ * bti + ki_ * btr
    ccr = cr_ref[0, 0, 0]
    cci = ci_ref[0, 0, 0]
    b_a = jnp.where(first, bbr, -bbi)
    b_b = jnp.where(first, -bbi, -bbr)
    c_a = jnp.where(first, ccr, cci)
    c_b = jnp.where(first, -cci, ccr)
    b_ri_a = jnp.where(first, bbr, bbi)
    b_ri_b = jnp.where(first, -bbi, bbr)
    c_n_a = jnp.where(first, ccr, -cci)
    c_n_b = jnp.where(first, -cci, -ccr)

    th = t_len * n_h
    tt = lax.broadcasted_iota(jnp.int32, (t_len, 1), 0).astype(F32)
    half = 0.5 * t_len
    n_out = jnp.where(rev, tt, (t_len - 1.0) - tt)
    n_in = jnp.where(rev, t_len - tt, tt + 1.0)
    powers = []
    for n in (sgn * (half - tt), sgn * (tt - half), n_out, n_in):
        powers.extend(power(n))
    rep = (lax.broadcasted_iota(jnp.int32, (th, t_len), 0) // n_h
           == lax.broadcasted_iota(jnp.int32, (th, t_len), 1)).astype(BF16)
    wide = sum(_dot(rep, part) for part in _bf16_pieces(jnp.concatenate(powers, axis=1), 3))
    pw = [wide[:, k * 2 * n_p:(k + 1) * 2 * n_p] for k in range(8)]

    def expand(pr, pi_, xa, xb):
        return (pr * jnp.concatenate([xa] * t_len, axis=0)
                + pi_ * jnp.concatenate([xb] * t_len, axis=0))

    a_cat = expand(pw[0], pw[1], b_a, b_b)
    c_cat = expand(pw[2], pw[3], c_a, c_b)
    a_hi, a_lo = _bf16_pieces(a_cat, 2)
    c_hi, c_lo = _bf16_pieces(c_cat, 2)
    mt = _dot_t(a_hi, c_hi) + (_dot_t(a_hi, c_lo) + _dot_t(a_lo, c_hi))
    rs = lax.broadcasted_iota(jnp.int32, (th, th), 0) // n_h
    ct = lax.broadcasted_iota(jnp.int32, (th, th), 1) // n_h
    keep = (ct - rs) * jnp.where(rev, -1, 1) >= 0
    mt_ref[0, 0, 0] = jnp.where(keep, mt, 0.0).astype(mt_ref.dtype)

    bp_ref[0, 0, 0] = expand(pw[4], pw[5], b_ri_a, b_ri_b)
    cp_ref[0, 0, 0] = expand(pw[6], pw[7], c_n_a, c_n_b)
    ltr, lti = power(jnp.full((1, 1), float(t_len), F32))
    lam_ref[0, 0, 0, 0:1, :] = ltr
    lam_ref[0, 0, 0, 1:2, :] = jnp.where(first, -lti, lti)


def _s5_prep(a_re, a_im, log_dt, b_re, b_im, c_re, c_im, t_len):
    depth, _, g, p, h = b_re.shape
    dup = lambda t: jnp.concatenate([t, t], axis=-1)
    are = dup(a_re).reshape(depth, 2, g, 1, 2 * p)
    aim = dup(a_im).reshape(depth, 2, g, 1, 2 * p)
    ldt = log_dt.reshape(depth, 2, g, 1, 1)
    btr = dup(jnp.swapaxes(b_re, -1, -2))
    bti = dup(jnp.swapaxes(b_im, -1, -2))
    ccr = dup(c_re)
    cci = dup(c_im)
    th = t_len * h
    row = lambda r, c: pl.BlockSpec((1, 1, 1, r, c), lambda l, d, k: (l, d, k, 0, 0))
    return pl.pallas_call(
        functools.partial(_s5_prep_kernel, t_len=t_len, n_h=h, n_p=p),
        grid=(depth, 2, g),
        in_specs=[row(1, 2 * p), row(1, 2 * p), row(1, 1), row(h, 2 * p), row(h, 2 * p),
                  row(h, 2 * p), row(h, 2 * p)],
        out_specs=[row(th, th), row(th, 2 * p), row(th, 2 * p), row(2, 2 * p)],
        out_shape=[jax.ShapeDtypeStruct((depth, 2, g, th, th), BF16),
                   jax.ShapeDtypeStruct((depth, 2, g, th, 2 * p), F32),
                   jax.ShapeDtypeStruct((depth, 2, g, th, 2 * p), F32),
                   jax.ShapeDtypeStruct((depth, 2, g, 2, 2 * p), F32)],
        compiler_params=_cp(("arbitrary", "arbitrary", "arbitrary")),
        name="s5_prep",
    )(are, aim, ldt, btr, bti, ccr, cci)


def _s5_kernel(u_ref, mt_ref, bp_ref, cp_ref, lam_ref, y_ref, hin_ref, *, n_ctx_chunks, n_p, n_batch):
    ub = u_ref[0]
    uf = ub.astype(F32)
    per = 8 // n_batch
    n_tiles = hin_ref.shape[1]
    n_chunks = n_tiles * per
    row = lax.broadcasted_iota(jnp.int32, (8, 2 * n_p), 0)
    hloc = [_dot(uf, bp_ref[d, 0], precision=HI).reshape(n_tiles, 8, 2 * n_p) for d in range(2)]
    orders = [list(range(n_chunks)),
              list(range(n_ctx_chunks - 1, -1, -1)) + list(range(n_chunks - 1, n_ctx_chunks - 1, -1))]
    h = [jnp.zeros((8, 2 * n_p), F32) for _ in range(2)]
    tile = [None, None]
    for i in range(n_chunks):
        for d in range(2):
            g, q = divmod(orders[d][i], per)
            placed = h[d] if q == 0 else pltpu.roll(h[d], q * n_batch, axis=0)
            keep = (row >= q * n_batch) & (row < (q + 1) * n_batch)
            tile[d] = jnp.where(keep, placed, 0.0 if i % per == 0 else tile[d])
            if i % per == per - 1:
                hin_ref[d, g] = tile[d]
            loc = hloc[d][g] if q == 0 else pltpu.roll(hloc[d][g], 8 - q * n_batch, axis=0)
            h[d] = (h[d] * lam_ref[d, 0, 0:1, :] + pltpu.roll(h[d], n_p, axis=1) * lam_ref[d, 0, 1:2, :]
                    + loc)
    y = _dot(ub, mt_ref[0, 0]) + _dot(ub, mt_ref[1, 0])
    for d in range(2):
        hin = hin_ref[d].reshape(n_tiles * 8, 2 * n_p)
        y = y + _dot_t(hin, cp_ref[d, 0], precision=HI)
    y_ref[0] = y


def _s5_scan(u_g, mt, bp, cp, lam, n_ctx_chunks, n_batch):
    g, r, th = u_g.shape
    p2 = bp.shape[-1]
    per = 8 // n_batch
    assert 8 % n_batch == 0 and n_ctx_chunks % per == 0 and r % 8 == 0
    return pl.pallas_call(
        functools.partial(_s5_kernel, n_ctx_chunks=n_ctx_chunks, n_p=p2 // 2, n_batch=n_batch),
        grid=(g,),
        in_specs=[pl.BlockSpec((1, r, th), lambda k: (k, 0, 0)),
                  pl.BlockSpec((2, 1, th, th), lambda k: (0, k, 0, 0)),
                  pl.BlockSpec((2, 1, th, p2), lambda k: (0, k, 0, 0)),
                  pl.BlockSpec((2, 1, th, p2), lambda k: (0, k, 0, 0)),
                  pl.BlockSpec((2, 1, 2, p2), lambda k: (0, k, 0, 0))],
        out_specs=pl.BlockSpec((1, r, th), lambda k: (k, 0, 0)),
        out_shape=jax.ShapeDtypeStruct((g, r, th), F32),
        scratch_shapes=[pltpu.VMEM((2, r // 8, 8, p2), F32)],
        compiler_params=_cp(("arbitrary",)),
        name="s5_scan",
    )(u_g, mt, bp, cp, lam)


def _s5_to_groups(s_in, n_batch, seq, ctx_len, g, h, t_len):
    n_lat = n_batch * seq
    lat = s_in[:n_lat].reshape(n_batch, seq // t_len, t_len, g, h)
    cx = s_in[n_lat:].reshape(n_batch, ctx_len // t_len, t_len, g, h)
    al = jnp.concatenate([cx, lat], axis=1)
    al = jnp.transpose(al, (3, 1, 0, 2, 4))
    return al.reshape(g, al.shape[1] * n_batch, t_len * h)


def _s5_from_groups(y_g, n_batch, seq, ctx_len, g, h, t_len):
    n_chunks = y_g.shape[1] // n_batch
    y = y_g.reshape(g, n_chunks, n_batch, t_len, h)
    y = jnp.transpose(y, (2, 1, 3, 0, 4))
    ncc = ctx_len // t_len
    cx = y[:, :ncc].reshape(n_batch * ctx_len, g * h)
    lat = y[:, ncc:].reshape(n_batch * seq, g * h)
    return jnp.concatenate([lat, cx], axis=0)


def _merge_kernel(al_ref, ac_ref, s_ref, y_ref, u_ref, d_ref, gw_ref, gb_ref, n_ref, o_ref, *,
                  wa, ws, n_lat_blocks):
    def rms(t, g):
        return t * lax.rsqrt(jnp.mean(t * t, axis=-1, keepdims=True) + EPS) * g

    y = _gelu(y_ref[...] + u_ref[...] * d_ref[...])
    z = y * jax.nn.sigmoid(_dot(y.astype(BF16), gw_ref[...]) + gb_ref[...])
    a = jnp.where(pl.program_id(0) < n_lat_blocks, al_ref[...], ac_ref[...])
    o_ref[:, :wa] = rms(a, n_ref[:, :wa]).astype(o_ref.dtype)
    o_ref[:, wa:wa + ws] = rms(s_ref[...], n_ref[:, wa:wa + ws]).astype(o_ref.dtype)
    o_ref[:, wa + ws:] = rms(z, n_ref[:, wa + ws:]).astype(o_ref.dtype)


def _merge(attn_lat, attn_ctx, sgu, y5, proj, s5_d, glu_w, glu_b, out_norm, tm):
    wa = attn_lat.shape[1]
    n, ws = sgu.shape
    w5 = y5.shape[1]
    wt = wa + ws + w5
    u_blk = proj.shape[1] // w5 - 1
    nlb = attn_lat.shape[0] // tm
    return pl.pallas_call(
        functools.partial(_merge_kernel, wa=wa, ws=ws, n_lat_blocks=nlb),
        grid=(n // tm,),
        in_specs=[pl.BlockSpec((tm, wa), lambda i: (jnp.minimum(i, nlb - 1), 0)),
                  pl.BlockSpec((tm, wa), lambda i: (jnp.maximum(i - nlb, 0), 0)),
                  pl.BlockSpec((tm, ws), lambda i: (i, 0)),
                  pl.BlockSpec((tm, w5), lambda i: (i, 0)),
                  pl.BlockSpec((tm, w5), lambda i: (i, u_blk)),
                  pl.BlockSpec((1, w5), lambda i: (0, 0)),
                  pl.BlockSpec((w5, w5), lambda i: (0, 0)),
                  pl.BlockSpec((1, w5), lambda i: (0, 0)),
                  pl.BlockSpec((1, wt), lambda i: (0, 0))],
        out_specs=pl.BlockSpec((tm, wt), lambda i: (i, 0)),
        out_shape=jax.ShapeDtypeStruct((n, wt), BF16),
        compiler_params=_cp(("arbitrary",)),
        name="merge_norm",
    )(attn_lat, attn_ctx, sgu, y5, proj, s5_d.reshape(1, w5), glu_w.astype(BF16), glu_b.reshape(1, w5),
      out_norm.reshape(1, wt))


def _topk_cols(ss, k, payloads=None):
    n, t = ss[0].shape
    iota = lax.broadcasted_iota(jnp.int32, (n, t), 0).astype(F32)
    krow = lax.broadcasted_iota(jnp.int32, (k, t), 0)
    ss = list(ss)
    vals = [jnp.zeros((k, t), F32) for _ in ss]
    outs = [jnp.zeros((k, t), F32) for _ in ss]
    for it in range(k):
        for p in range(len(ss)):
            s = ss[p]
            m = jnp.max(s, axis=0, keepdims=True)
            pos = jnp.min(jnp.where(s == m, iota, float(n)), axis=0, keepdims=True)
            hit = iota == pos
            if payloads is None:
                sel = pos
            else:
                sel = jnp.max(jnp.where(hit, payloads[p], -1.0), axis=0, keepdims=True)
            vals[p] = jnp.where(krow == it, m, vals[p])
            outs[p] = jnp.where(krow == it, sel, outs[p])
            ss[p] = jnp.where(hit, -jnp.inf, s)
    return list(zip(vals, outs))


def _staircase_candidates(v1, i1, v2, i2, k, nkeys):
    s = math.isqrt(k)
    t = v1.shape[1]
    r = lax.broadcasted_iota(jnp.int32, (8, t), 0)
    sc, ex = [], []

    def emit(vs, es, lo, hi):
        ok = (r >= lo) & (r < hi)
        sc.append(jnp.where(ok, vs, -jnp.inf))
        ex.append(es)

    for b in range(s):
        na = k // (b + 1)
        for a0 in range(0, na, 8):
            emit(v1[a0:a0 + 8] + v2[b:b + 1], i1[a0:a0 + 8] * nkeys + i2[b:b + 1], 0, na - a0)
    for a in range(s):
        nb = k // (a + 1)
        for b0 in range((s // 8) * 8, nb, 8):
            if nb - b0 > max(s - b0, 0):
                emit(v1[a:a + 1] + v2[b0:b0 + 8], i1[a:a + 1] * nkeys + i2[b0:b0 + 8], s - b0, nb - b0)
    return jnp.concatenate(sc, axis=0), jnp.concatenate(ex, axis=0)


def _peer_sel_kernel(x_ref, g_ref, m_ref, wq_ref, k1_ref, k2_ref, ids_ref, gate_ref, q_scr, *,
                     heads, qdim, nkeys, topk, tl, row):
    x = x_ref[...]
    m = m_ref[0]
    h = x * lax.rsqrt(jnp.mean(x * x, axis=-1, keepdims=True) + EPS) * g_ref[...]
    h = (h * (1.0 + m[row + 1:row + 2]) + m[row:row + 1]).astype(BF16)
    q_scr[...] = _dot(h, wq_ref[...])
    half = qdim // 2

    def head_body(hh, carry):
        n_th = x_ref.shape[0] // tl
        scores = []
        for th in range(n_th):
            q1 = q_scr[pl.ds(th * tl, tl), pl.ds(pl.multiple_of(hh * qdim, qdim), half)]
            q2 = q_scr[pl.ds(th * tl, tl), pl.ds(pl.multiple_of(hh * qdim + half, half), half)]
            scores.append(_dot_t(k1_ref[hh], q1, precision=HI))
            scores.append(_dot_t(k2_ref[hh], q2, precision=HI))
        tops = _topk_cols(scores, topk)
        cands = [_staircase_candidates(*tops[2 * th], *tops[2 * th + 1], topk, float(nkeys))
                 for th in range(n_th)]
        finals = _topk_cols([c[0] for c in cands], topk, payloads=[c[1] for c in cands])
        for th in range(n_th):
            best, expert = finals[th]
            e = jnp.exp(best - best[0:1])
            gate = e / jnp.sum(e, axis=0, keepdims=True)
            rows = pl.ds(pl.multiple_of(hh * topk, topk), topk)
            ids_ref[rows, pl.ds(th * tl, tl)] = expert.astype(jnp.int32)
            gate_ref[rows, pl.ds(th * tl, tl)] = gate
        return carry

    lax.fori_loop(0, heads, head_body, 0)


def _peer_select(x, norm_g, mod, row, lat_blocks, wq, k1, k2, tm, tl=128):
    n, d = x.shape
    heads, nkeys, half = k1.shape
    qdim = 2 * half
    nsel = heads * PEER_TOPK
    nlb, bpb, nb = lat_blocks
    return pl.pallas_call(
        functools.partial(_peer_sel_kernel, heads=heads, qdim=qdim, nkeys=nkeys, topk=PEER_TOPK, tl=tl,
                          row=row),
        grid=(n // tm,),
        in_specs=[pl.BlockSpec((tm, d), lambda i: (i, 0)),
                  pl.BlockSpec((1, d), lambda i: (0, 0)),
                  pl.BlockSpec((1, mod.shape[1], d), lambda i: (_mod_row(i, nlb, bpb, nb), 0, 0)),
                  pl.BlockSpec((d, heads * qdim), lambda i: (0, 0)),
                  pl.BlockSpec((heads, nkeys, half), lambda i: (0, 0, 0)),
                  pl.BlockSpec((heads, nkeys, half), lambda i: (0, 0, 0))],
        out_specs=[pl.BlockSpec((nsel, tm), lambda i: (0, i)),
                   pl.BlockSpec((nsel, tm), lambda i: (0, i))],
        out_shape=[jax.ShapeDtypeStruct((nsel, n), jnp.int32),
                   jax.ShapeDtypeStruct((nsel, n), F32)],
        scratch_shapes=[pltpu.VMEM((tm, heads * qdim), F32)],
        compiler_params=_cp(("arbitrary",)),
        name="peer_select",
    )(x, norm_g.reshape(1, d), mod, wq, k1, k2)


def _pack_kernel(u_ref, v_ref, o_ref):
    hi = lax.bitcast_convert_type(u_ref[...].astype(BF16).astype(F32), jnp.uint32)
    lo = lax.bitcast_convert_type(v_ref[...].astype(BF16).astype(F32), jnp.uint32)
    o_ref[...] = hi | (lo >> 16)


def _pack_tables(u, v, tr=256):
    depth, e, d = u.shape
    blk = pl.BlockSpec((1, tr, d), lambda l, i: (l, i, 0))
    return pl.pallas_call(
        _pack_kernel,
        grid=(depth, e // tr),
        in_specs=[blk, blk],
        out_specs=blk,
        out_shape=jax.ShapeDtypeStruct((depth, e, d), jnp.uint32),
        compiler_params=_cp(("arbitrary", "arbitrary")),
        name="pack_tables",
    )(u, v)


def _peer_ffn_kernel(ids_ref, idn_ref, g_ref, x_ref, n_ref, m_ref, tab_ref, o_ref, buf0_ref, buf1_ref, sem_ref,
                     *, tg, nsel, row, layer):
    i = pl.program_id(0)
    nsteps = pl.num_programs(0)
    m = m_ref[0]
    eye = (lax.broadcasted_iota(jnp.int32, (nsel, nsel), 0)
           == lax.broadcasted_iota(jnp.int32, (nsel, nsel), 1))

    def row_copy(e, buf, slot, k, j):
        return pltpu.make_async_copy(tab_ref.at[layer, pl.ds(e, 1)], buf.at[pl.ds(k * nsel + j, 1)],
                                     sem_ref.at[slot, k])

    def token_wait(buf, slot, k):
        pltpu.make_async_copy(tab_ref.at[layer, pl.ds(0, nsel)], buf.at[pl.ds(k * nsel, nsel)],
                              sem_ref.at[slot, k]).wait()

    @pl.when(i == 0)
    def _():
        for k in range(tg):
            def body(j, carry, k=k):
                row_copy(ids_ref[k, j], buf0_ref, 0, k, j).start()
                return carry
            lax.fori_loop(0, nsel, body, 0)

    def step(cur, nxt, cs, ns):
        for k in range(tg):
            token_wait(cur, cs, k)
            for j in range(nsel):
                row_copy(idn_ref[k, j], nxt, ns, k, j).start()
            xk = x_ref[k:k + 1, :]
            hk = xk * lax.rsqrt(jnp.mean(xk * xk, axis=-1, keepdims=True) + EPS) * n_ref[...]
            hk = hk * (1.0 + m[row + 1:row + 2]) + m[row:row + 1]
            w = cur[k * nsel:(k + 1) * nsel, :]
            uf = lax.bitcast_convert_type(w & jnp.uint32(0xFFFF0000), F32)
            act = _gelu(jnp.sum(uf * hk, axis=1, keepdims=True))
            gcol = jnp.sum(jnp.where(eye, g_ref[k:k + 1, :], 0.0), axis=1, keepdims=True)
            vf = lax.bitcast_convert_type(w << 16, F32)
            out = jnp.sum(vf * (gcol * act), axis=0, keepdims=True)
            o_ref[k:k + 1, :] = xk + m[row + 2:row + 3] * out

        @pl.when(i == nsteps - 1)
        def _():
            for k in range(tg):
                token_wait(nxt, ns, k)

    @pl.when(i % 2 == 0)
    def _():
        step(buf0_ref, buf1_ref, 0, 1)

    @pl.when(i % 2 == 1)
    def _():
        step(buf1_ref, buf0_ref, 1, 0)


def _peer_ffn(x, ids, gates, norm_g, mod, row, tab, layer, lat_blocks, tg=8):
    n, d = x.shape
    nsel = ids.shape[1]
    nlb, bpb, nb = lat_blocks
    steps = n // tg
    return pl.pallas_call(
        functools.partial(_peer_ffn_kernel, tg=tg, nsel=nsel, row=row, layer=layer),
        grid=(steps,),
        in_specs=[pl.BlockSpec((tg, nsel), lambda i: (i, 0), memory_space=pltpu.SMEM),
                  pl.BlockSpec((tg, nsel), lambda i: (jnp.minimum(i + 1, steps - 1), 0),
                               memory_space=pltpu.SMEM),
                  pl.BlockSpec((tg, nsel), lambda i: (i, 0)),
                  pl.BlockSpec((tg, d), lambda i: (i, 0)),
                  pl.BlockSpec((1, d), lambda i: (0, 0)),
                  pl.BlockSpec((1, mod.shape[1], d), lambda i: (_mod_row(i, nlb, bpb, nb), 0, 0)),
                  pl.BlockSpec(memory_space=pl.ANY)],
        out_specs=pl.BlockSpec((tg, d), lambda i: (i, 0)),
        out_shape=jax.ShapeDtypeStruct((n, d), F32),
        scratch_shapes=[pltpu.VMEM((tg * nsel, d), jnp.uint32),
                        pltpu.VMEM((tg * nsel, d), jnp.uint32),
                        pltpu.SemaphoreType.DMA((2, tg))],
        compiler_params=_cp(("arbitrary",)),
        name="peer_ffn",
    )(ids, ids, gates, x, norm_g.reshape(1, d), mod, tab)


def _final_norm_kernel(x_ref, g_ref, o_ref):
    x = x_ref[...]
    o_ref[...] = x * lax.rsqrt(jnp.mean(x * x, axis=-1, keepdims=True) + EPS) * g_ref[...]


def _final_norm(x, g, n_rows, tm):
    d = x.shape[1]
    return pl.pallas_call(
        _final_norm_kernel,
        grid=(n_rows // tm,),
        in_specs=[pl.BlockSpec((tm, d), lambda i: (i, 0)),
                  pl.BlockSpec((1, d), lambda i: (0, 0))],
        out_specs=pl.BlockSpec((tm, d), lambda i: (i, 0)),
        out_shape=jax.ShapeDtypeStruct((n_rows, d), F32),
        compiler_params=_cp(("arbitrary",)),
        name="final_norm",
    )(x, g.reshape(1, d))


def kernel(x, c, ctx, c_ctx, ada_w, ada_b, norm_mix, norm_ffn, w_in, w_out, out_norm, na_rpb, sgu_w, sgu_b,
           sgu_norm, s5_a_re, s5_a_im, s5_log_dt, s5_b_re, s5_b_im, s5_c_re, s5_c_im, s5_d, s5_glu_w,
           s5_glu_b, peer_wq, peer_k1, peer_k2, peer_u, peer_v, final_norm):
    n_batch, seq, d = x.shape
    ctx_len = ctx.shape[1]
    depth = ada_w.shape[0]
    n_mod = ada_w.shape[2] // d
    in_w = w_in.shape[2]
    sgu_width = sgu_norm.shape[-1]
    s5_width = s5_d.shape[-1]
    na_width = (in_w - 2 * sgu_width - s5_width) // 3
    na_heads = na_rpb.shape[1]
    dh = na_width // na_heads
    kr_full = (na_rpb.shape[2] + 1) // 2
    wc = (na_rpb.shape[3] + 1) // 2
    rows = seq // GRID_W
    kr = min(kr_full, rows)
    _, _, s5_g, s5_p, s5_h = s5_b_re.shape
    n_lat = n_batch * seq
    n_tok = n_lat + n_batch * ctx_len

    tm = 256
    lat_blocks = (n_lat // tm, seq // tm, n_batch)
    tg = 8
    grp_blocks = (n_lat // tg, seq // tg, n_batch)

    cond = jnp.concatenate([c, c_ctx[None], jnp.zeros((8 - n_batch - 1, d), F32)], axis=0)
    mods = _ada_mod(cond, ada_w, ada_b).reshape(depth, 8, n_mod, d)

    rb_rows = 8
    types, rb_type, bstart, co, valid = _na_tables(rows, GRID_W, kr, wc, rb_rows)
    types_ref = np.where(types == 2 * kr - 1, 2 * kr_full - 1, types + (kr_full - kr))
    s5_ops = _s5_prep(s5_a_re, s5_a_im, s5_log_dt, s5_b_re, s5_b_im, s5_c_re, s5_c_im, S5_T)
    tabs = _pack_tables(peer_u, peer_v)

    xs = jnp.concatenate([x.reshape(n_lat, d), ctx.reshape(n_batch * ctx_len, d)], axis=0)
    for l in range(depth):
        mod = mods[l]
        h = _norm_mod(xs, norm_mix[l], mod, 0, lat_blocks, tm)
        proj = _matmul(h, w_in[l].astype(BF16), _tile(n_tok, 512, 8), _tile(in_w, 1152, 128))

        bias = _na_bias(na_rpb[l], types_ref, co, valid)
        attn_lat = _na_attention(proj, bias, jnp.asarray(rb_type), jnp.asarray(bstart), n_batch, seq, ctx_len,
                                 na_heads, dh, kr, rb_rows)
        attn_ctx = _ctx_attention(proj, n_batch, seq, ctx_len, na_heads, dh)

        sgu = _sgu(proj, sgu_w[l], sgu_b[l], sgu_norm[l], 3 * na_width // sgu_width, tm)

        s_in = proj[:, in_w - s5_width:].astype(BF16)
        u_g = _s5_to_groups(s_in, n_batch, seq, ctx_len, s5_g, s5_h, S5_T)
        y_g = _s5_scan(u_g, s5_ops[0][l], s5_ops[1][l], s5_ops[2][l], s5_ops[3][l], ctx_len // S5_T, n_batch)
        y5 = _s5_from_groups(y_g, n_batch, seq, ctx_len, s5_g, s5_h, S5_T)

        merged = _merge(attn_lat, attn_ctx, sgu, y5, proj, s5_d[l], s5_glu_w[l], s5_glu_b[l], out_norm[l], tm)
        xs = _matmul_res(merged, w_out[l].astype(BF16), xs, mod, 2, lat_blocks, tm, _tile(d, 1024, 128))

        if l == depth - 1:
            xs = xs[:n_lat]
        ids_t, gates_t = _peer_select(xs, norm_ffn[l], mod, 3, lat_blocks, peer_wq[l].astype(BF16),
                                      peer_k1[l], peer_k2[l], tm)
        xs = _peer_ffn(xs, ids_t.T, gates_t.T, norm_ffn[l], mod, 3, tabs, l, grp_blocks, tg)

    out = _final_norm(xs, final_norm, n_lat, tm)
    return out.reshape(n_batch, seq, d)
```

```python
import functools
import math

import numpy as np
import jax
import jax.numpy as jnp
from jax import lax
from jax.experimental import pallas as pl
from jax.experimental.pallas import tpu as pltpu

F32 = jnp.float32
BF16 = jnp.bfloat16
EPS = 1e-6
GRID_W = 64
PEER_TOPK = 16
S5_T = 64
NEG = -1e30
HI = lax.Precision.HIGHEST
VMEM_LIMIT = 56 * 1024 * 1024


def _cp(sem, vmem=VMEM_LIMIT):
    return pltpu.CompilerParams(dimension_semantics=sem, vmem_limit_bytes=vmem)


def _tile(n, cap, unit):
    return max(t for t in range(unit, min(n, cap) + 1, unit) if n % t == 0)


def _gelu(x):
    return 0.5 * x * (1.0 + lax.erf(x * (1.0 / math.sqrt(2.0))))


def _dot_t(a, b, precision=None):
    return lax.dot_general(a, b, (((1,), (1,)), ((), ())), precision=precision,
                           preferred_element_type=F32)


def _dot(a, b, precision=None):
    return lax.dot_general(a, b, (((1,), (0,)), ((), ())), precision=precision,
                           preferred_element_type=F32)


def _bf16_pieces(x, n):
    out = []
    for _ in range(n):
        p = x.astype(BF16)
        out.append(p)
        x = x - p.astype(F32)
    return out


def _mod_row(i, n_lat_blocks, blocks_per_batch, n_batch):
    return jnp.where(i < n_lat_blocks, i // blocks_per_batch, n_batch)


def _ada_kernel(c_ref, w_ref, b_ref, o_ref):
    c = c_ref[...]
    s = (c * jax.nn.sigmoid(c)).astype(BF16)
    o_ref[0] = _dot(s, w_ref[0].astype(BF16)) + b_ref[0]


def _ada_mod(cond, ada_w, ada_b, tn=1024):
    depth, d, nm = ada_w.shape
    rows = cond.shape[0]
    return pl.pallas_call(
        _ada_kernel,
        grid=(depth, nm // tn),
        in_specs=[pl.BlockSpec((rows, d), lambda l, j: (0, 0)),
                  pl.BlockSpec((1, d, tn), lambda l, j: (l, 0, j)),
                  pl.BlockSpec((1, 1, tn), lambda l, j: (l, 0, j))],
        out_specs=pl.BlockSpec((1, rows, tn), lambda l, j: (l, 0, j)),
        out_shape=jax.ShapeDtypeStruct((depth, rows, nm), F32),
        compiler_params=_cp(("arbitrary", "arbitrary")),
        name="ada_mod",
    )(cond, ada_w, ada_b.reshape(depth, 1, nm))


def _norm_mod_kernel(x_ref, g_ref, m_ref, o_ref, *, row):
    x = x_ref[...]
    m = m_ref[0]
    y = x * lax.rsqrt(jnp.mean(x * x, axis=-1, keepdims=True) + EPS) * g_ref[...]
    o_ref[...] = (y * (1.0 + m[row + 1:row + 2]) + m[row:row + 1]).astype(o_ref.dtype)


def _norm_mod(x, g, mod, row, lat_blocks, tm):
    n, d = x.shape
    nlb, bpb, nb = lat_blocks
    return pl.pallas_call(
        functools.partial(_norm_mod_kernel, row=row),
        grid=(n // tm,),
        in_specs=[pl.BlockSpec((tm, d), lambda i: (i, 0)),
                  pl.BlockSpec((1, d), lambda i: (0, 0)),
                  pl.BlockSpec((1, mod.shape[1], d), lambda i: (_mod_row(i, nlb, bpb, nb), 0, 0))],
        out_specs=pl.BlockSpec((tm, d), lambda i: (i, 0)),
        out_shape=jax.ShapeDtypeStruct((n, d), BF16),
        compiler_params=_cp(("arbitrary",)),
        name="norm_mod",
    )(x, g.reshape(1, d), mod)


def _mm_kernel(a_ref, w_ref, o_ref):
    o_ref[...] = _dot(a_ref[...], w_ref[...])


def _mm_res_kernel(a_ref, w_ref, r_ref, m_ref, o_ref, *, row):
    o_ref[...] = r_ref[...] + m_ref[0][row:row + 1] * _dot(a_ref[...], w_ref[...])


def _matmul(a, w, tm, tn):
    m, k = a.shape
    nn = w.shape[1]
    return pl.pallas_call(
        _mm_kernel,
        grid=(nn // tn, m // tm),
        in_specs=[pl.BlockSpec((tm, k), lambda j, i: (i, 0)),
                  pl.BlockSpec((k, tn), lambda j, i: (0, j))],
        out_specs=pl.BlockSpec((tm, tn), lambda j, i: (i, j)),
        out_shape=jax.ShapeDtypeStruct((m, nn), F32),
        compiler_params=_cp(("arbitrary", "arbitrary")),
        name="matmul",
    )(a, w)


def _matmul_res(a, w, res, mod, row, lat_blocks, tm, tn):
    m, k = a.shape
    nn = w.shape[1]
    nlb, bpb, nb = lat_blocks
    return pl.pallas_call(
        functools.partial(_mm_res_kernel, row=row),
        grid=(nn // tn, m // tm),
        in_specs=[pl.BlockSpec((tm, k), lambda j, i: (i, 0)),
                  pl.BlockSpec((k, tn), lambda j, i: (0, j)),
                  pl.BlockSpec((tm, tn), lambda j, i: (i, j)),
                  pl.BlockSpec((1, mod.shape[1], tn), lambda j, i: (_mod_row(i, nlb, bpb, nb), 0, j))],
        out_specs=pl.BlockSpec((tm, tn), lambda j, i: (i, j)),
        out_shape=jax.ShapeDtypeStruct((m, nn), F32),
        compiler_params=_cp(("arbitrary", "arbitrary")),
        name="matmul_res",
    )(a, w, res, mod)


def _na_tables(rows, width, kr, wc, rb_rows):
    band = rb_rows + kr
    nrb = rows // rb_rows
    masked = 2 * kr - 1
    ro = np.full((nrb, rb_rows, band), masked, dtype=np.int32)
    bstart = np.zeros((nrb,), dtype=np.int32)
    for rb in range(nrb):
        bs = int(np.clip(rb_rows * rb - kr // 2, 0, rows - band))
        bstart[rb] = bs
        for dlt in range(rb_rows):
            r = rb_rows * rb + dlt
            rs = int(np.clip(r - kr // 2, 0, rows - kr))
            for j in range(kr):
                ro[rb, dlt, rs + j - bs] = rs + j - r + (kr - 1)
    types, inv = np.unique(ro.reshape(nrb, -1), axis=0, return_inverse=True)
    types = types.reshape(-1, rb_rows, band)
    col = np.arange(width)
    cs = np.clip(col - wc // 2, 0, width - wc)
    kc = np.arange(width)
    valid = (kc[None, :] >= cs[:, None]) & (kc[None, :] < cs[:, None] + wc)
    co = np.clip(kc[None, :] - col[:, None] + (wc - 1), 0, 2 * wc - 2)
    return types, inv.reshape(-1).astype(np.int32), bstart, co, valid


def _na_bias(rpb, types, co, valid):
    h = rpb.shape[0]
    w = co.shape[0]
    tb = jnp.where(valid[None, None], rpb[:, :, co], NEG)
    tb = jnp.concatenate([tb, jnp.full((h, 1, w, w), NEG, F32)], axis=1)
    dense = tb[:, types]
    nt, rbr, band = types.shape
    dense = jnp.transpose(dense, (1, 0, 2, 4, 3, 5))
    return dense.reshape(nt, h, rbr * w, band * w)


def _na_kernel(type_ref, bstart_ref, q_ref, k_ref, v_ref, kc_ref, vc_ref, bias_ref, o_ref, *,
               band_tokens, width, scale):
    rb = pl.program_id(2)
    start = pl.multiple_of(bstart_ref[rb] * width, width)
    q = q_ref[...].astype(BF16)
    kb = k_ref[pl.ds(start, band_tokens), :].astype(BF16)
    vb = v_ref[pl.ds(start, band_tokens), :].astype(BF16)
    s_loc = _dot_t(q, kb) * scale + bias_ref[0, 0]
    s_ctx = _dot_t(q, kc_ref[...].astype(BF16)) * scale
    m = jnp.maximum(jnp.max(s_loc, axis=-1, keepdims=True), jnp.max(s_ctx, axis=-1, keepdims=True))
    p_loc = jnp.exp(s_loc - m)
    p_ctx = jnp.exp(s_ctx - m)
    den = jnp.sum(p_loc, axis=-1, keepdims=True) + jnp.sum(p_ctx, axis=-1, keepdims=True)
    o = _dot(p_loc.astype(BF16), vb) + _dot(p_ctx.astype(BF16), vc_ref[...].astype(BF16))
    o_ref[...] = o / den


def _na_attention(proj, bias, rb_type, bstart, n_batch, seq, ctx_len, heads, dh, kr, rb_rows):
    width = GRID_W
    nrb = seq // (rb_rows * width)
    qt = rb_rows * width
    band_tokens = (rb_rows + kr) * width
    ctx_blk0 = n_batch * seq // ctx_len
    grid_spec = pltpu.PrefetchScalarGridSpec(
        num_scalar_prefetch=2,
        grid=(n_batch, heads, nrb),
        in_specs=[
            pl.BlockSpec((qt, dh), lambda b, h, r, t, s: (b * nrb + r, h)),
            pl.BlockSpec((seq, dh), lambda b, h, r, t, s: (b, heads + h)),
            pl.BlockSpec((seq, dh), lambda b, h, r, t, s: (b, 2 * heads + h)),
            pl.BlockSpec((ctx_len, dh), lambda b, h, r, t, s: (ctx_blk0 + b, heads + h)),
            pl.BlockSpec((ctx_len, dh), lambda b, h, r, t, s: (ctx_blk0 + b, 2 * heads + h)),
            pl.BlockSpec((1, 1, qt, band_tokens), lambda b, h, r, t, s: (t[r], h, 0, 0)),
        ],
        out_specs=pl.BlockSpec((qt, dh), lambda b, h, r, t, s: (b * nrb + r, h)),
    )
    return pl.pallas_call(
        functools.partial(_na_kernel, band_tokens=band_tokens, width=width, scale=dh ** -0.5),
        grid_spec=grid_spec,
        out_shape=jax.ShapeDtypeStruct((n_batch * seq, heads * dh), F32),
        compiler_params=_cp(("arbitrary", "arbitrary", "arbitrary")),
        name="na_attention",
    )(rb_type, bstart, proj, proj, proj, proj, proj, bias)


def _ctx_attn_kernel(q_ref, k_ref, v_ref, o_ref, *, scale):
    s = _dot_t(q_ref[...].astype(BF16), k_ref[...].astype(BF16)) * scale
    m = jnp.max(s, axis=-1, keepdims=True)
    p = jnp.exp(s - m)
    den = jnp.sum(p, axis=-1, keepdims=True)
    o_ref[...] = _dot(p.astype(BF16), v_ref[...].astype(BF16)) / den


def _ctx_attention(proj, n_batch, seq, ctx_len, heads, dh):
    ctx_blk0 = n_batch * seq // ctx_len
    return pl.pallas_call(
        functools.partial(_ctx_attn_kernel, scale=dh ** -0.5),
        grid=(n_batch, heads),
        in_specs=[pl.BlockSpec((ctx_len, dh), lambda b, h: (ctx_blk0 + b, h)),
                  pl.BlockSpec((ctx_len, dh), lambda b, h: (ctx_blk0 + b, heads + h)),
                  pl.BlockSpec((ctx_len, dh), lambda b, h: (ctx_blk0 + b, 2 * heads + h))],
        out_specs=pl.BlockSpec((ctx_len, dh), lambda b, h: (b, h)),
        out_shape=jax.ShapeDtypeStruct((n_batch * ctx_len, heads * dh), F32),
        compiler_params=_cp(("arbitrary", "arbitrary")),
        name="ctx_attention",
    )(proj, proj, proj)


def _sgu_kernel(su_ref, sv_ref, w_ref, b_ref, g_ref, o_ref, *, chunk, heads, hd):
    u = _gelu(su_ref[...])
    gv = _gelu(sv_ref[...])
    v = (gv * lax.rsqrt(jnp.mean(gv * gv, axis=-1, keepdims=True) + EPS) * g_ref[...]).astype(BF16)
    for n in range(u.shape[0] // chunk):
        rs = slice(n * chunk, (n + 1) * chunk)
        for h in range(heads):
            cs = slice(h * hd, (h + 1) * hd)
            mixed = _dot(w_ref[h], v[rs, cs]) + b_ref[h]
            o_ref[rs, cs] = u[rs, cs] * mixed


def _sgu(proj, sgu_w, sgu_b, sgu_norm, col_blk, tm):
    n = proj.shape[0]
    heads, chunk, _ = sgu_w.shape
    sw = sgu_norm.shape[-1]
    hd = sw // heads
    bias = jnp.broadcast_to(sgu_b[:, :, None], (heads, chunk, hd))
    return pl.pallas_call(
        functools.partial(_sgu_kernel, chunk=chunk, heads=heads, hd=hd),
        grid=(n // tm,),
        in_specs=[pl.BlockSpec((tm, sw), lambda i: (i, col_blk)),
                  pl.BlockSpec((tm, sw), lambda i: (i, col_blk + 1)),
                  pl.BlockSpec((heads, chunk, chunk), lambda i: (0, 0, 0)),
                  pl.BlockSpec((heads, chunk, hd), lambda i: (0, 0, 0)),
                  pl.BlockSpec((1, sw), lambda i: (0, 0))],
        out_specs=pl.BlockSpec((tm, sw), lambda i: (i, 0)),
        out_shape=jax.ShapeDtypeStruct((n, sw), F32),
        compiler_params=_cp(("arbitrary",)),
        name="sgu",
    )(proj, proj, sgu_w.astype(BF16), bias, sgu_norm.reshape(1, sw))


def _s5_prep_kernel(are_ref, aim_ref, ldt_ref, btr_ref, bti_ref, cr_ref, ci_ref,
                    mt_ref, bp_ref, cp_ref, lam_ref, *, t_len, n_h, n_p):
    rev = pl.program_id(1) == 1
    sgn = jnp.where(rev, -1.0, 1.0)
    are = jnp.minimum(are_ref[0, 0, 0], -1e-4)
    aim = aim_ref[0, 0, 0]
    dt = jnp.exp(ldt_ref[0, 0, 0])
    lr = are * dt
    li = aim * dt
    first = lax.broadcasted_iota(jnp.int32, (1, 2 * n_p), 1) < n_p

    def power(n):
        mag = jnp.exp(n * lr)
        ang = n * li
        return mag * jnp.cos(ang), mag * jnp.sin(ang)

    lbr = jnp.exp(lr) * jnp.cos(li)
    lbi = jnp.exp(lr) * jnp.sin(li)
    den = are * are + aim * aim
    nr = lbr - 1.0
    kr_ = (nr * are + lbi * aim) / den
    ki_ = (lbi * are - nr * aim) / den
    btr = btr_ref[0, 0, 0]
    bti = bti_ref[0, 0, 0]
    bbr = kr_ * btr - ki_ * bti
    bbi = kr_ * bti + ki_ * btr
    ccr = cr_ref[0, 0, 0]
    cci = ci_ref[0, 0, 0]
    b_a = jnp.where(first, bbr, -bbi)
    b_b = jnp.where(first, -bbi, -bbr)
    c_a = jnp.where(first, ccr, cci)
    c_b = jnp.where(first, -cci, ccr)
    b_ri_a = jnp.where(first, bbr, bbi)
    b_ri_b = jnp.where(first, -bbi, bbr)
    c_n_a = jnp.where(first, ccr, -cci)
    c_n_b = jnp.where(first, -cci, -ccr)

    th = t_len * n_h
    tt = lax.broadcasted_iota(jnp.int32, (t_len, 1), 0).astype(F32)
    half = 0.5 * t_len
    n_out = jnp.where(rev, tt, (t_len - 1.0) - tt)
    n_in = jnp.where(rev, t_len - tt, tt + 1.0)
    powers = []
    for n in (sgn * (half - tt), sgn * (tt - half), n_out, n_in):
        powers.extend(power(n))
    rep = (lax.broadcasted_iota(jnp.int32, (th, t_len), 0) // n_h
           == lax.broadcasted_iota(jnp.int32, (th, t_len), 1)).astype(BF16)
    wide = sum(_dot(rep, part) for part in _bf16_pieces(jnp.concatenate(powers, axis=1), 3))
    pw = [wide[:, k * 2 * n_p:(k + 1) * 2 * n_p] for k in range(8)]

    def expand(pr, pi_, xa, xb):
        return (pr * jnp.concatenate([xa] * t_len, axis=0)
                + pi_ * jnp.concatenate([xb] * t_len, axis=0))

    a_cat = expand(pw[0], pw[1], b_a, b_b)
    c_cat = expand(pw[2], pw[3], c_a, c_b)
    a_hi, a_lo = _bf16_pieces(a_cat, 2)
    c_hi, c_lo = _bf16_pieces(c_cat, 2)
    mt = _dot_t(a_hi, c_hi) + (_dot_t(a_hi, c_lo) + _dot_t(a_lo, c_hi))
    rs = lax.broadcasted_iota(jnp.int32, (th, th), 0) // n_h
    ct = lax.broadcasted_iota(jnp.int32, (th, th), 1) // n_h
    keep = (ct - rs) * jnp.where(rev, -1, 1) >= 0
    mt_ref[0, 0, 0] = jnp.where(keep, mt, 0.0).astype(mt_ref.dtype)

    bp_ref[0, 0, 0] = expand(pw[4], pw[5], b_ri_a, b_ri_b)
    cp_ref[0, 0, 0] = expand(pw[6], pw[7], c_n_a, c_n_b)
    ltr, lti = power(jnp.full((1, 1), float(t_len), F32))
    lam_ref[0, 0, 0, 0:1, :] = ltr
    lam_ref[0, 0, 0, 1:2, :] = jnp.where(first, -lti, lti)


def _s5_prep(a_re, a_im, log_dt, b_re, b_im, c_re, c_im, t_len):
    depth, _, g, p, h = b_re.shape
    dup = lambda t: jnp.concatenate([t, t], axis=-1)
    are = dup(a_re).reshape(depth, 2, g, 1, 2 * p)
    aim = dup(a_im).reshape(depth, 2, g, 1, 2 * p)
    ldt = log_dt.reshape(depth, 2, g, 1, 1)
    btr = dup(jnp.swapaxes(b_re, -1, -2))
    bti = dup(jnp.swapaxes(b_im, -1, -2))
    ccr = dup(c_re)
    cci = dup(c_im)
    th = t_len * h
    row = lambda r, c: pl.BlockSpec((1, 1, 1, r, c), lambda l, d, k: (l, d, k, 0, 0))
    return pl.pallas_call(
        functools.partial(_s5_prep_kernel, t_len=t_len, n_h=h, n_p=p),
        grid=(depth, 2, g),
        in_specs=[row(1, 2 * p), row(1, 2 * p), row(1, 1), row(h, 2 * p), row(h, 2 * p),
                  row(h, 2 * p), row(h, 2 * p)],
        out_specs=[row(th, th), row(th, 2 * p), row(th, 2 * p), row(2, 2 * p)],
        out_shape=[jax.ShapeDtypeStruct((depth, 2, g, th, th), BF16),
                   jax.ShapeDtypeStruct((depth, 2, g, th, 2 * p), F32),
                   jax.ShapeDtypeStruct((depth, 2, g, th, 2 * p), F32),
                   jax.ShapeDtypeStruct((depth, 2, g, 2, 2 * p), F32)],
        compiler_params=_cp(("arbitrary", "arbitrary", "arbitrary")),
        name="s5_prep",
    )(are, aim, ldt, btr, bti, ccr, cci)


def _s5_kernel(u_ref, mt_ref, bp_ref, cp_ref, lam_ref, y_ref, hin_ref, *, n_ctx_chunks, n_p, n_batch):
    ub = u_ref[0]
    uf = ub.astype(F32)
    per = 8 // n_batch
    n_tiles = hin_ref.shape[1]
    n_chunks = n_tiles * per
    row = lax.broadcasted_iota(jnp.int32, (8, 2 * n_p), 0)
    hloc = [_dot(uf, bp_ref[d, 0], precision=HI).reshape(n_tiles, 8, 2 * n_p) for d in range(2)]
    orders = [list(range(n_chunks)),
              list(range(n_ctx_chunks - 1, -1, -1)) + list(range(n_chunks - 1, n_ctx_chunks - 1, -1))]
    h = [jnp.zeros((8, 2 * n_p), F32) for _ in range(2)]
    tile = [None, None]
    for i in range(n_chunks):
        for d in range(2):
            g, q = divmod(orders[d][i], per)
            placed = h[d] if q == 0 else pltpu.roll(h[d], q * n_batch, axis=0)
            keep = (row >= q * n_batch) & (row < (q + 1) * n_batch)
            tile[d] = jnp.where(keep, placed, 0.0 if i % per == 0 else tile[d])
            if i % per == per - 1:
                hin_ref[d, g] = tile[d]
            loc = hloc[d][g] if q == 0 else pltpu.roll(hloc[d][g], 8 - q * n_batch, axis=0)
            h[d] = (h[d] * lam_ref[d, 0, 0:1, :] + pltpu.roll(h[d], n_p, axis=1) * lam_ref[d, 0, 1:2, :]
                    + loc)
    y = _dot(ub, mt_ref[0, 0]) + _dot(ub, mt_ref[1, 0])
    for d in range(2):
        hin = hin_ref[d].reshape(n_tiles * 8, 2 * n_p)
        y = y + _dot_t(hin, cp_ref[d, 0], precision=HI)
    y_ref[0] = y


def _s5_scan(u_g, mt, bp, cp, lam, n_ctx_chunks, n_batch):
    g, r, th = u_g.shape
    p2 = bp.shape[-1]
    per = 8 // n_batch
    assert 8 % n_batch == 0 and n_ctx_chunks % per == 0 and r % 8 == 0
    return pl.pallas_call(
        functools.partial(_s5_kernel, n_ctx_chunks=n_ctx_chunks, n_p=p2 // 2, n_batch=n_batch),
        grid=(g,),
        in_specs=[pl.BlockSpec((1, r, th), lambda k: (k, 0, 0)),
                  pl.BlockSpec((2, 1, th, th), lambda k: (0, k, 0, 0)),
                  pl.BlockSpec((2, 1, th, p2), lambda k: (0, k, 0, 0)),
                  pl.BlockSpec((2, 1, th, p2), lambda k: (0, k, 0, 0)),
                  pl.BlockSpec((2, 1, 2, p2), lambda k: (0, k, 0, 0))],
        out_specs=pl.BlockSpec((1, r, th), lambda k: (k, 0, 0)),
        out_shape=jax.ShapeDtypeStruct((g, r, th), F32),
        scratch_shapes=[pltpu.VMEM((2, r // 8, 8, p2), F32)],
        compiler_params=_cp(("arbitrary",)),
        name="s5_scan",
    )(u_g, mt, bp, cp, lam)


def _s5_to_groups(s_in, n_batch, seq, ctx_len, g, h, t_len):
    n_lat = n_batch * seq
    lat = s_in[:n_lat].reshape(n_batch, seq // t_len, t_len, g, h)
    cx = s_in[n_lat:].reshape(n_batch, ctx_len // t_len, t_len, g, h)
    al = jnp.concatenate([cx, lat], axis=1)
    al = jnp.transpose(al, (3, 1, 0, 2, 4))
    return al.reshape(g, al.shape[1] * n_batch, t_len * h)


def _s5_from_groups(y_g, n_batch, seq, ctx_len, g, h, t_len):
    n_chunks = y_g.shape[1] // n_batch
    y = y_g.reshape(g, n_chunks, n_batch, t_len, h)
    y = jnp.transpose(y, (2, 1, 3, 0, 4))
    ncc = ctx_len // t_len
    cx = y[:, :ncc].reshape(n_batch * ctx_len, g * h)
    lat = y[:, ncc:].reshape(n_batch * seq, g * h)
    return jnp.concatenate([lat, cx], axis=0)


def _merge_kernel(a_ref, s_ref, y_ref, u_ref, d_ref, gw_ref, gb_ref, n_ref, o_ref, *, wa, ws):
    def rms(t, g):
        return t * lax.rsqrt(jnp.mean(t * t, axis=-1, keepdims=True) + EPS) * g

    y = _gelu(y_ref[...] + u_ref[...] * d_ref[...])
    z = y * jax.nn.sigmoid(_dot(y.astype(BF16), gw_ref[...]) + gb_ref[...])
    o_ref[:, :wa] = rms(a_ref[...], n_ref[:, :wa]).astype(o_ref.dtype)
    o_ref[:, wa:wa + ws] = rms(s_ref[...], n_ref[:, wa:wa + ws]).astype(o_ref.dtype)
    o_ref[:, wa + ws:] = rms(z, n_ref[:, wa + ws:]).astype(o_ref.dtype)


def _merge(attn, sgu, y5, proj, s5_d, glu_w, glu_b, out_norm, tm):
    n, wa = attn.shape
    ws = sgu.shape[1]
    w5 = y5.shape[1]
    wt = wa + ws + w5
    u_blk = proj.shape[1] // w5 - 1
    return pl.pallas_call(
        functools.partial(_merge_kernel, wa=wa, ws=ws),
        grid=(n // tm,),
        in_specs=[pl.BlockSpec((tm, wa), lambda i: (i, 0)),
                  pl.BlockSpec((tm, ws), lambda i: (i, 0)),
                  pl.BlockSpec((tm, w5), lambda i: (i, 0)),
                  pl.BlockSpec((tm, w5), lambda i: (i, u_blk)),
                  pl.BlockSpec((1, w5), lambda i: (0, 0)),
                  pl.BlockSpec((w5, w5), lambda i: (0, 0)),
                  pl.BlockSpec((1, w5), lambda i: (0, 0)),
                  pl.BlockSpec((1, wt), lambda i: (0, 0))],
        out_specs=pl.BlockSpec((tm, wt), lambda i: (i, 0)),
        out_shape=jax.ShapeDtypeStruct((n, wt), BF16),
        compiler_params=_cp(("arbitrary",)),
        name="merge_norm",
    )(attn, sgu, y5, proj, s5_d.reshape(1, w5), glu_w.astype(BF16), glu_b.reshape(1, w5),
      out_norm.reshape(1, wt))


def _topk_cols(ss, k, payloads=None):
    n, t = ss[0].shape
    iota = lax.broadcasted_iota(jnp.int32, (n, t), 0).astype(F32)
    krow = lax.broadcasted_iota(jnp.int32, (k, t), 0)
    ss = list(ss)
    vals = [jnp.zeros((k, t), F32) for _ in ss]
    outs = [jnp.zeros((k, t), F32) for _ in ss]
    for it in range(k):
        for p in range(len(ss)):
            s = ss[p]
            m = jnp.max(s, axis=0, keepdims=True)
            pos = jnp.min(jnp.where(s == m, iota, float(n)), axis=0, keepdims=True)
            hit = iota == pos
            if payloads is None:
                sel = pos
            else:
                sel = jnp.max(jnp.where(hit, payloads[p], -1.0), axis=0, keepdims=True)
            vals[p] = jnp.where(krow == it, m, vals[p])
            outs[p] = jnp.where(krow == it, sel, outs[p])
            ss[p] = jnp.where(hit, -jnp.inf, s)
    return list(zip(vals, outs))


def _staircase_candidates(v1, i1, v2, i2, k, nkeys):
    s = math.isqrt(k)
    t = v1.shape[1]
    r = lax.broadcasted_iota(jnp.int32, (8, t), 0)
    sc, ex = [], []

    def emit(vs, es, lo, hi):
        ok = (r >= lo) & (r < hi)
        sc.append(jnp.where(ok, vs, -jnp.inf))
        ex.append(es)

    for b in range(s):
        na = k // (b + 1)
        for a0 in range(0, na, 8):
            emit(v1[a0:a0 + 8] + v2[b:b + 1], i1[a0:a0 + 8] * nkeys + i2[b:b + 1], 0, na - a0)
    for a in range(s):
        nb = k // (a + 1)
        for b0 in range((s // 8) * 8, nb, 8):
            if nb - b0 > max(s - b0, 0):
                emit(v1[a:a + 1] + v2[b0:b0 + 8], i1[a:a + 1] * nkeys + i2[b0:b0 + 8], s - b0, nb - b0)
    return jnp.concatenate(sc, axis=0), jnp.concatenate(ex, axis=0)


def _peer_sel_kernel(h_ref, wq_ref, k1_ref, k2_ref, ids_ref, gate_ref, q_scr, *, heads, qdim, nkeys, topk, tl):
    q_scr[...] = _dot(h_ref[...], wq_ref[...])
    half = qdim // 2

    def head_body(hh, carry):
        n_th = h_ref.shape[0] // tl
        scores = []
        for th in range(n_th):
            q1 = q_scr[pl.ds(th * tl, tl), pl.ds(pl.multiple_of(hh * qdim, qdim), half)]
            q2 = q_scr[pl.ds(th * tl, tl), pl.ds(pl.multiple_of(hh * qdim + half, half), half)]
            scores.append(_dot_t(k1_ref[hh], q1, precision=HI))
            scores.append(_dot_t(k2_ref[hh], q2, precision=HI))
        tops = _topk_cols(scores, topk)
        cands = [_staircase_candidates(*tops[2 * th], *tops[2 * th + 1], topk, float(nkeys))
                 for th in range(n_th)]
        finals = _topk_cols([c[0] for c in cands], topk, payloads=[c[1] for c in cands])
        for th in range(n_th):
            best, expert = finals[th]
            e = jnp.exp(best - best[0:1])
            gate = e / jnp.sum(e, axis=0, keepdims=True)
            rows = pl.ds(pl.multiple_of(hh * topk, topk), topk)
            ids_ref[rows, pl.ds(th * tl, tl)] = expert.astype(jnp.int32)
            gate_ref[rows, pl.ds(th * tl, tl)] = gate
        return carry

    lax.fori_loop(0, heads, head_body, 0)


def _peer_select(h, wq, k1, k2, tm, tl=128):
    n, d = h.shape
    heads, nkeys, half = k1.shape
    qdim = 2 * half
    nsel = heads * PEER_TOPK
    return pl.pallas_call(
        functools.partial(_peer_sel_kernel, heads=heads, qdim=qdim, nkeys=nkeys, topk=PEER_TOPK, tl=tl),
        grid=(n // tm,),
        in_specs=[pl.BlockSpec((tm, d), lambda i: (i, 0)),
                  pl.BlockSpec((d, heads * qdim), lambda i: (0, 0)),
                  pl.BlockSpec((heads, nkeys, half), lambda i: (0, 0, 0)),
                  pl.BlockSpec((heads, nkeys, half), lambda i: (0, 0, 0))],
        out_specs=[pl.BlockSpec((nsel, tm), lambda i: (0, i)),
                   pl.BlockSpec((nsel, tm), lambda i: (0, i))],
        out_shape=[jax.ShapeDtypeStruct((nsel, n), jnp.int32),
                   jax.ShapeDtypeStruct((nsel, n), F32)],
        scratch_shapes=[pltpu.VMEM((tm, heads * qdim), F32)],
        compiler_params=_cp(("arbitrary",)),
        name="peer_select",
    )(h, wq, k1, k2)


def _pack_tables(u, v):
    ub = lax.bitcast_convert_type(u.astype(BF16), jnp.uint16).astype(jnp.uint32)
    vb = lax.bitcast_convert_type(v.astype(BF16), jnp.uint16).astype(jnp.uint32)
    return (ub << 16) | vb


def _peer_ffn_kernel(ids_ref, idn_ref, g_ref, x_ref, n_ref, m_ref, tab_ref, o_ref, buf0_ref, buf1_ref, sem_ref,
                     *, tg, nsel, row):
    i = pl.program_id(0)
    nsteps = pl.num_programs(0)
    m = m_ref[0]
    eye = (lax.broadcasted_iota(jnp.int32, (nsel, nsel), 0)
           == lax.broadcasted_iota(jnp.int32, (nsel, nsel), 1))

    def row_copy(e, buf, slot, k, j):
        return pltpu.make_async_copy(tab_ref.at[pl.ds(e, 1)], buf.at[pl.ds(k * nsel + j, 1)],
                                     sem_ref.at[slot, k])

    def token_wait(buf, slot, k):
        pltpu.make_async_copy(tab_ref.at[pl.ds(0, nsel)], buf.at[pl.ds(k * nsel, nsel)],
                              sem_ref.at[slot, k]).wait()

    @pl.when(i == 0)
    def _():
        for k in range(tg):
            def body(j, carry, k=k):
                row_copy(ids_ref[k, j], buf0_ref, 0, k, j).start()
                return carry
            lax.fori_loop(0, nsel, body, 0)

    def step(cur, nxt, cs, ns):
        for k in range(tg):
            token_wait(cur, cs, k)
            for j in range(nsel):
                row_copy(idn_ref[k, j], nxt, ns, k, j).start(priority=j % 2)
            xk = x_ref[k:k + 1, :]
            hk = xk * lax.rsqrt(jnp.mean(xk * xk, axis=-1, keepdims=True) + EPS) * n_ref[...]
            hk = hk * (1.0 + m[row + 1:row + 2]) + m[row:row + 1]
            w = cur[k * nsel:(k + 1) * nsel, :]
            uf = lax.bitcast_convert_type(w & jnp.uint32(0xFFFF0000), F32)
            act = _gelu(jnp.sum(uf * hk, axis=1, keepdims=True))
            gcol = jnp.sum(jnp.where(eye, g_ref[k:k + 1, :], 0.0), axis=1, keepdims=True)
            vf = lax.bitcast_convert_type(w << 16, F32)
            out = jnp.sum(vf * (gcol * act), axis=0, keepdims=True)
            o_ref[k:k + 1, :] = xk + m[row + 2:row + 3] * out

        @pl.when(i == nsteps - 1)
        def _():
            for k in range(tg):
                token_wait(nxt, ns, k)

    @pl.when(i % 2 == 0)
    def _():
        step(buf0_ref, buf1_ref, 0, 1)

    @pl.when(i % 2 == 1)
    def _():
        step(buf1_ref, buf0_ref, 1, 0)


def _peer_ffn(x, ids, gates, norm_g, mod, row, tab, lat_blocks, tg=8):
    n, d = x.shape
    nsel = ids.shape[1]
    nlb, bpb, nb = lat_blocks
    steps = n // tg
    return pl.pallas_call(
        functools.partial(_peer_ffn_kernel, tg=tg, nsel=nsel, row=row),
        grid=(steps,),
        in_specs=[pl.BlockSpec((tg, nsel), lambda i: (i, 0), memory_space=pltpu.SMEM),
                  pl.BlockSpec((tg, nsel), lambda i: (jnp.minimum(i + 1, steps - 1), 0),
                               memory_space=pltpu.SMEM),
                  pl.BlockSpec((tg, nsel), lambda i: (i, 0)),
                  pl.BlockSpec((tg, d), lambda i: (i, 0)),
                  pl.BlockSpec((1, d), lambda i: (0, 0)),
                  pl.BlockSpec((1, mod.shape[1], d), lambda i: (_mod_row(i, nlb, bpb, nb), 0, 0)),
                  pl.BlockSpec(memory_space=pl.ANY)],
        out_specs=pl.BlockSpec((tg, d), lambda i: (i, 0)),
        out_shape=jax.ShapeDtypeStruct((n, d), F32),
        scratch_shapes=[pltpu.VMEM((tg * nsel, d), jnp.uint32),
                        pltpu.VMEM((tg * nsel, d), jnp.uint32),
                        pltpu.SemaphoreType.DMA((2, tg))],
        compiler_params=_cp(("arbitrary",)),
        name="peer_ffn",
    )(ids, ids, gates, x, norm_g.reshape(1, d), mod, tab)


def _final_norm_kernel(x_ref, g_ref, o_ref):
    x = x_ref[...]
    o_ref[...] = x * lax.rsqrt(jnp.mean(x * x, axis=-1, keepdims=True) + EPS) * g_ref[...]


def _final_norm(x, g, n_rows, tm):
    d = x.shape[1]
    return pl.pallas_call(
        _final_norm_kernel,
        grid=(n_rows // tm,),
        in_specs=[pl.BlockSpec((tm, d), lambda i: (i, 0)),
                  pl.BlockSpec((1, d), lambda i: (0, 0))],
        out_specs=pl.BlockSpec((tm, d), lambda i: (i, 0)),
        out_shape=jax.ShapeDtypeStruct((n_rows, d), F32),
        compiler_params=_cp(("arbitrary",)),
        name="final_norm",
    )(x, g.reshape(1, d))


def kernel(x, c, ctx, c_ctx, ada_w, ada_b, norm_mix, norm_ffn, w_in, w_out, out_norm, na_rpb, sgu_w, sgu_b,
           sgu_norm, s5_a_re, s5_a_im, s5_log_dt, s5_b_re, s5_b_im, s5_c_re, s5_c_im, s5_d, s5_glu_w,
           s5_glu_b, peer_wq, peer_k1, peer_k2, peer_u, peer_v, final_norm):
    n_batch, seq, d = x.shape
    ctx_len = ctx.shape[1]
    depth = ada_w.shape[0]
    n_mod = ada_w.shape[2] // d
    in_w = w_in.shape[2]
    sgu_width = sgu_norm.shape[-1]
    s5_width = s5_d.shape[-1]
    na_width = (in_w - 2 * sgu_width - s5_width) // 3
    na_heads = na_rpb.shape[1]
    dh = na_width // na_heads
    kr_full = (na_rpb.shape[2] + 1) // 2
    wc = (na_rpb.shape[3] + 1) // 2
    rows = seq // GRID_W
    kr = min(kr_full, rows)
    _, _, s5_g, s5_p, s5_h = s5_b_re.shape
    n_lat = n_batch * seq
    n_tok = n_lat + n_batch * ctx_len

    tm = 256
    lat_blocks = (n_lat // tm, seq // tm, n_batch)
    tg = 8
    grp_blocks = (n_lat // tg, seq // tg, n_batch)

    cond = jnp.concatenate([c, c_ctx[None], jnp.zeros((8 - n_batch - 1, d), F32)], axis=0)
    mods = _ada_mod(cond, ada_w, ada_b).reshape(depth, 8, n_mod, d)

    rb_rows = 8
    types, rb_type, bstart, co, valid = _na_tables(rows, GRID_W, kr, wc, rb_rows)
    types_ref = np.where(types == 2 * kr - 1, 2 * kr_full - 1, types + (kr_full - kr))
    s5_ops = _s5_prep(s5_a_re, s5_a_im, s5_log_dt, s5_b_re, s5_b_im, s5_c_re, s5_c_im, S5_T)

    xs = jnp.concatenate([x.reshape(n_lat, d), ctx.reshape(n_batch * ctx_len, d)], axis=0)
    for l in range(depth):
        mod = mods[l]
        h = _norm_mod(xs, norm_mix[l], mod, 0, lat_blocks, tm)
        proj = _matmul(h, w_in[l].astype(BF16), _tile(n_tok, 512, 8), _tile(in_w, 1152, 128))

        bias = _na_bias(na_rpb[l], types_ref, co, valid)
        attn_lat = _na_attention(proj, bias, jnp.asarray(rb_type), jnp.asarray(bstart), n_batch, seq, ctx_len,
                                 na_heads, dh, kr, rb_rows)
        attn_ctx = _ctx_attention(proj, n_batch, seq, ctx_len, na_heads, dh)
        attn = jnp.concatenate([attn_lat, attn_ctx], axis=0)

        sgu = _sgu(proj, sgu_w[l], sgu_b[l], sgu_norm[l], 3 * na_width // sgu_width, tm)

        s_in = proj[:, in_w - s5_width:].astype(BF16)
        u_g = _s5_to_groups(s_in, n_batch, seq, ctx_len, s5_g, s5_h, S5_T)
        y_g = _s5_scan(u_g, s5_ops[0][l], s5_ops[1][l], s5_ops[2][l], s5_ops[3][l], ctx_len // S5_T, n_batch)
        y5 = _s5_from_groups(y_g, n_batch, seq, ctx_len, s5_g, s5_h, S5_T)

        merged = _merge(attn, sgu, y5, proj, s5_d[l], s5_glu_w[l], s5_glu_b[l], out_norm[l], tm)
        xs = _matmul_res(merged, w_out[l].astype(BF16), xs, mod, 2, lat_blocks, tm, _tile(d, 1024, 128))

        if l == depth - 1:
            xs = xs[:n_lat]
        hf = _norm_mod(xs, norm_ffn[l], mod, 3, lat_blocks, tm)
        ids_t, gates_t = _peer_select(hf, peer_wq[l].astype(BF16), peer_k1[l], peer_k2[l], tm)
        tab = _pack_tables(peer_u[l], peer_v[l])
        xs = _peer_ffn(xs, ids_t.T, gates_t.T, norm_ffn[l], mod, 3, tab, grp_blocks, tg)

    out = _final_norm(xs, final_norm, n_lat, tm)
    return out.reshape(n_batch, seq, d)
```
